```python
import jax, jax.numpy as jnp
from jax import lax

D_MODEL = 1024
BATCH = 8
SEQ = 2048
DEPTH = 4

GRID_W = 64
CTX_LEN = 256
NORM_EPS = 1e-6
N_MOD = 6
MASK_VALUE = -1e30

POOL_WINDOWS = (2, 4, 8, 16)
N_POOL_GROUPS = len(POOL_WINDOWS)
POOL_GROUP_DIM = D_MODEL // 8
D_POOL = N_POOL_GROUPS * POOL_GROUP_DIM

HEAD_DIM = 64
N_Q_HEADS = D_MODEL // 128
N_KV_HEADS = N_Q_HEADS // 4
GQA_GROUP = N_Q_HEADS // N_KV_HEADS
D_ATTN = N_Q_HEADS * HEAD_DIM
D_KV = N_KV_HEADS * HEAD_DIM
WINDOW = 128
ATTN_BLOCK = 128
ROPE_THETA = 10000.0
ROPE_AXIS_DIM = HEAD_DIM // 2
ROPE_FREQS = ROPE_AXIS_DIM // 2

CHUNK = 128
N_SG_GROUPS = 4
D_SG = D_MODEL // 2
SG_GROUP_DIM = D_SG // N_SG_GROUPS

N_BRANCHES = 3
OFF_Q = D_POOL
OFF_K = OFF_Q + D_ATTN
OFF_V = OFF_K + D_KV
OFF_U = OFF_V + D_KV
OFF_SV = OFF_U + D_SG
OFF_GATE = OFF_SV + D_SG
D_IN = OFF_GATE + N_BRANCHES * D_MODEL

D_FF = -(-8 * D_MODEL // (3 * 256)) * 256

kernel_name = 'hybrid_pool_swa_gmlp_dit'


def rmsnorm(x, gain):
    xf = x.astype(jnp.float32)
    y = xf * lax.rsqrt(jnp.mean(xf * xf, axis=-1, keepdims=True) + NORM_EPS)
    return (y * gain.astype(jnp.float32)).astype(x.dtype)


def modulate(x, gain, shift, scale):
    return rmsnorm(x, gain) * (1 + scale) + shift


def heads(t, n):
    return t.reshape(t.shape[:-1] + (n, HEAD_DIM))


def grid_rope_tables(rows):
    row = jnp.repeat(jnp.arange(rows), GRID_W).astype(jnp.float32)
    col = jnp.tile(jnp.arange(GRID_W), rows).astype(jnp.float32)
    inv_freq = ROPE_THETA ** (-jnp.arange(ROPE_FREQS, dtype=jnp.float32) / ROPE_FREQS)
    ang_r = row[:, None] * inv_freq[None, :]
    ang_c = col[:, None] * inv_freq[None, :]
    return (jnp.cos(ang_r), jnp.sin(ang_r), jnp.cos(ang_c), jnp.sin(ang_c))


def _rotate(xp, cos, sin):
    x1, x2 = xp[..., :ROPE_FREQS], xp[..., ROPE_FREQS:]
    cos = cos[None, :, None, :]
    sin = sin[None, :, None, :]
    return jnp.concatenate([x1 * cos - x2 * sin, x2 * cos + x1 * sin], axis=-1)


def rope_2d(t, tables):
    cos_r, sin_r, cos_c, sin_c = tables
    tf = t.astype(jnp.float32)
    out = jnp.concatenate([_rotate(tf[..., :ROPE_AXIS_DIM], cos_r, sin_r),
                           _rotate(tf[..., ROPE_AXIS_DIM:], cos_c, sin_c)], axis=-1)
    return out.astype(t.dtype)


def multiscale_pool(a, w_pool, pool_scale):
    bsz, length = a.shape[:2]
    af = a.astype(jnp.float32)
    csum = jnp.concatenate([jnp.zeros_like(af[:, :1]), jnp.cumsum(af, axis=1)], axis=1)
    pos = jnp.arange(length)
    means = []
    for g, w in enumerate(POOL_WINDOWS):
        lo = jnp.clip(pos - w // 2, 0, length)
        hi = jnp.clip(pos + (w - w // 2), 0, length)
        cg = csum[..., g * POOL_GROUP_DIM:(g + 1) * POOL_GROUP_DIM]
        cnt = (hi - lo).astype(jnp.float32)[:, None]
        means.append((cg[:, hi] - cg[:, lo]) / cnt)
    pooled = jnp.stack(means, axis=2) - af.reshape(bsz, length, N_POOL_GROUPS, POOL_GROUP_DIM)
    mixed = jnp.einsum('blgc,gcd->blgd', pooled.astype(a.dtype), w_pool)
    return mixed.reshape(bsz, length, D_POOL) * pool_scale


def banded(t, n_blocks):
    bsz = t.shape[0]
    tp = jnp.pad(t, ((0, 0), (ATTN_BLOCK, ATTN_BLOCK), (0, 0), (0, 0)))
    tb = tp.reshape(bsz, n_blocks + 2, ATTN_BLOCK, N_KV_HEADS, HEAD_DIM)
    return jnp.concatenate([tb[:, :-2], tb[:, 1:-1], tb[:, 2:]], axis=2)


def sink_column(sink, shape):
    s = sink.astype(jnp.float32).reshape(N_KV_HEADS, GQA_GROUP, 1, 1)
    return jnp.broadcast_to(s, shape[:-1] + (1,))


def window_attention_with_context(q, k, v, k_ctx, v_ctx, sink):
    bsz, length = q.shape[:2]
    nb = length // ATTN_BLOCK
    scale = HEAD_DIM ** -0.5
    qb = q.reshape(bsz, nb, ATTN_BLOCK, N_KV_HEADS, GQA_GROUP, HEAD_DIM)
    kb, vb = banded(k, nb), banded(v, nb)
    s_loc = jnp.einsum('bnqhgd,bnkhd->bnhgqk', qb, kb, preferred_element_type=jnp.float32) * scale
    q_pos = jnp.arange(nb)[:, None, None] * ATTN_BLOCK + jnp.arange(ATTN_BLOCK)[None, :, None]
    k_pos = jnp.arange(nb)[:, None, None] * ATTN_BLOCK - ATTN_BLOCK + jnp.arange(3 * ATTN_BLOCK)[None, None, :]
    valid = (jnp.abs(q_pos - k_pos) <= WINDOW) & (k_pos >= 0) & (k_pos < length)
    s_loc = jnp.where(valid[None, :, None, None], s_loc, MASK_VALUE)
    s_ctx = jnp.einsum('bnqhgd,bchd->bnhgqc', qb, k_ctx, preferred_element_type=jnp.float32) * scale
    logits = jnp.concatenate([s_loc, s_ctx, sink_column(sink, s_loc.shape)], axis=-1)
    p = jax.nn.softmax(logits, axis=-1).astype(v.dtype)
    n_loc = 3 * ATTN_BLOCK
    n_ctx = k_ctx.shape[1]
    out = (jnp.einsum('bnhgqk,bnkhd->bnqhgd', p[..., :n_loc], vb)
           + jnp.einsum('bnhgqc,bchd->bnqhgd', p[..., n_loc:n_loc + n_ctx], v_ctx))
    return out.reshape(bsz, length, D_ATTN)


def context_self_attention(q, k, v, sink):
    bsz, n = q.shape[:2]
    scale = HEAD_DIM ** -0.5
    qg = q.reshape(bsz, n, N_KV_HEADS, GQA_GROUP, HEAD_DIM)
    s = jnp.einsum('bqhgd,bkhd->bhgqk', qg, k, preferred_element_type=jnp.float32) * scale
    p = jax.nn.softmax(jnp.concatenate([s, sink_column(sink, s.shape)], axis=-1), axis=-1)
    out = jnp.einsum('bhgqk,bkhd->bqhgd', p[..., :n].astype(v.dtype), v)
    return out.reshape(bsz, n, D_ATTN)


def spatial_gating(u, v, v_gain, w_spatial, b_spatial):
    bsz, length = u.shape[:2]
    nc = length // CHUNK
    vn = rmsnorm(v, v_gain).reshape(bsz, nc, CHUNK, N_SG_GROUPS, SG_GROUP_DIM)
    mixed = jnp.einsum('gpr,bnrgc->bnpgc', w_spatial, vn) + b_spatial.T[:, :, None]
    return u * mixed.reshape(bsz, length, D_SG)


def mixer_merge(z, y_attn, w_pool, pool_scale, sg_v_gain, w_spatial, b_spatial,
                w_br_pool, w_br_attn, w_br_sg, w_out):
    y_pool = multiscale_pool(z[..., :OFF_Q], w_pool, pool_scale)
    y_sg = spatial_gating(jax.nn.gelu(z[..., OFF_U:OFF_SV]), jax.nn.gelu(z[..., OFF_SV:OFF_GATE]),
                          sg_v_gain, w_spatial, b_spatial)
    g = jax.nn.sigmoid(z[..., OFF_GATE:].reshape(z.shape[:-1] + (N_BRANCHES, D_MODEL)))
    y = (g[..., 0, :] * (y_pool @ w_br_pool)
         + g[..., 1, :] * (y_attn @ w_br_attn)
         + g[..., 2, :] * (y_sg @ w_br_sg))
    return y @ w_out


def swiglu(h, w_in, w_out):
    gu = h @ w_in
    return (jax.nn.silu(gu[..., :D_FF]) * gu[..., D_FF:]) @ w_out


def setup_inputs(seed: int = 0) -> dict:
    key = jax.random.key(seed)
    ks = jax.random.split(key, 24)
    f32 = jnp.float32

    def nrm(k, shape, scale):
        return jax.random.normal(k, shape, f32) * scale

    def gain(k, shape):
        return 1.0 + 0.02 * jax.random.normal(k, shape, f32)

    return {
        'x': nrm(ks[0], (BATCH, SEQ, D_MODEL), 1.0),
        'c': nrm(ks[1], (BATCH, D_MODEL), 1.0),
        'ctx': nrm(ks[2], (BATCH, CTX_LEN, D_MODEL), 1.0),
        'c_ctx': nrm(ks[3], (D_MODEL,), 1.0),
        'w_mod': nrm(ks[4], (DEPTH, D_MODEL, N_MOD * D_MODEL), 0.5 * D_MODEL ** -0.5),
        'b_mod': nrm(ks[5], (DEPTH, N_MOD * D_MODEL), 0.02),
        'norm1_gain': gain(ks[6], (DEPTH, D_MODEL)),
        'norm2_gain': gain(ks[7], (DEPTH, D_MODEL)),
        'w_in': nrm(ks[8], (DEPTH, D_MODEL, D_IN), D_MODEL ** -0.5),
        'w_pool': nrm(ks[9], (DEPTH, N_POOL_GROUPS, POOL_GROUP_DIM, POOL_GROUP_DIM), POOL_GROUP_DIM ** -0.5),
        'pool_scale': gain(ks[10], (DEPTH, D_POOL)),
        'attn_sink': nrm(ks[11], (DEPTH, N_Q_HEADS), 0.5),
        'sg_v_gain': gain(ks[12], (DEPTH, D_SG)),
        'w_spatial': nrm(ks[13], (DEPTH, N_SG_GROUPS, CHUNK, CHUNK), CHUNK ** -0.5),
        'b_spatial': gain(ks[14], (DEPTH, N_SG_GROUPS, CHUNK)),
        'w_br_pool': nrm(ks[15], (DEPTH, D_POOL, D_MODEL), D_POOL ** -0.5),
        'w_br_attn': nrm(ks[16], (DEPTH, D_ATTN, D_MODEL), D_ATTN ** -0.5),
        'w_br_sg': nrm(ks[17], (DEPTH, D_SG, D_MODEL), D_SG ** -0.5),
        'w_out': nrm(ks[18], (DEPTH, D_MODEL, D_MODEL), D_MODEL ** -0.5),
        'w_ffn_in': nrm(ks[19], (DEPTH, D_MODEL, 2 * D_FF), D_MODEL ** -0.5),
        'w_ffn_out': nrm(ks[20], (DEPTH, D_FF, D_MODEL), D_FF ** -0.5),
        'final_gain': gain(ks[21], (D_MODEL,)),
    }


def reference(x, c, ctx, c_ctx, w_mod, b_mod, norm1_gain, norm2_gain, w_in, w_pool, pool_scale,
              attn_sink, sg_v_gain, w_spatial, b_spatial, w_br_pool, w_br_attn, w_br_sg, w_out,
              w_ffn_in, w_ffn_out, final_gain):
    rows = x.shape[1] // GRID_W
    rope = grid_rope_tables(rows)
    cx = ctx
    sc = jax.nn.silu(c)
    scc = jax.nn.silu(c_ctx)
    for i in range(DEPTH):
        last = i == DEPTH - 1
        mod_x = jnp.split((sc @ w_mod[i] + b_mod[i])[:, None, :], N_MOD, axis=-1)
        mod_c = jnp.split((scc @ w_mod[i] + b_mod[i])[None, None, :], N_MOD, axis=-1)
        layer = (w_pool[i], pool_scale[i], sg_v_gain[i], w_spatial[i], b_spatial[i],
                 w_br_pool[i], w_br_attn[i], w_br_sg[i], w_out[i])

        hc = modulate(cx, norm1_gain[i], mod_c[0], mod_c[1])
        if last:
            zc = hc @ w_in[i, :, OFF_K:OFF_U]
            kc, vc = zc[..., :D_KV], zc[..., D_KV:]
        else:
            zc = hc @ w_in[i]
            kc, vc = zc[..., OFF_K:OFF_V], zc[..., OFF_V:OFF_U]
        kc, vc = heads(kc, N_KV_HEADS), heads(vc, N_KV_HEADS)

        hx = modulate(x, norm1_gain[i], mod_x[0], mod_x[1])
        zx = hx @ w_in[i]
        qx = rope_2d(heads(zx[..., OFF_Q:OFF_K], N_Q_HEADS), rope)
        kx = rope_2d(heads(zx[..., OFF_K:OFF_V], N_KV_HEADS), rope)
        vx = heads(zx[..., OFF_V:OFF_U], N_KV_HEADS)
        attn_x = window_attention_with_context(qx, kx, vx, kc, vc, attn_sink[i])
        x = x + mod_x[2] * mixer_merge(zx, attn_x, *layer)
        x = x + mod_x[5] * swiglu(modulate(x, norm2_gain[i], mod_x[3], mod_x[4]), w_ffn_in[i], w_ffn_out[i])

        if not last:
            attn_c = context_self_attention(heads(zc[..., OFF_Q:OFF_K], N_Q_HEADS), kc, vc, attn_sink[i])
            cx = cx + mod_c[2] * mixer_merge(zc, attn_c, *layer)
            cx = cx + mod_c[5] * swiglu(modulate(cx, norm2_gain[i], mod_c[3], mod_c[4]), w_ffn_in[i], w_ffn_out[i])
    return rmsnorm(x, final_gain)
```

```python
import functools

import jax
import jax.numpy as jnp
from jax import lax
from jax.experimental import pallas as pl
from jax.experimental.pallas import tpu as pltpu

F32 = jnp.float32
BF16 = jnp.bfloat16

GRID_W = 64
NORM_EPS = 1e-6
MASK_VALUE = -1e30
N_MOD = 6
POOL_WINDOWS = (2, 4, 8, 16)
HEAD_DIM = 64
ATTN_BLOCK = 128
ROPE_THETA = 10000.0
CHUNK = 128
N_SG_GROUPS = 4
N_BRANCHES = 3

LANES = 128
SUBLANES = 8
MXU_DIM = 256
VMEM_LIMIT_BYTES = 56 * 1024 * 1024

GROUP_LANES = 4 * HEAD_DIM
MOD_ROWS = 16


def _const_spec(shape):
    nd = len(shape)
    return pl.BlockSpec(shape, lambda *_: (0,) * nd, pipeline_mode=pl.Buffered(1))


def _params(n_axes=1):
    return pltpu.CompilerParams(dimension_semantics=("arbitrary",) * n_axes,
                                vmem_limit_bytes=VMEM_LIMIT_BYTES)


def _modnorm(xf, gain, shift, scale):
    ms = jnp.mean(xf * xf, axis=-1, keepdims=True)
    y = xf * lax.rsqrt(ms + NORM_EPS)
    return (y * gain) * (1.0 + scale) + shift


def _mod_kernel(c_ref, w_ref, b_ref, o_ref):
    c = c_ref[...]
    s = (c * jax.nn.sigmoid(c)).astype(BF16)
    o_ref[0] = jnp.dot(s, w_ref[0].astype(BF16), preferred_element_type=F32) + b_ref[0]


def _modulation(cc, w_mod, b_mod):
    depth, d, n = w_mod.shape
    tn = 1024
    return pl.pallas_call(
        _mod_kernel,
        grid=(depth, n // tn),
        in_specs=[
            pl.BlockSpec((MOD_ROWS, d), lambda l, j: (0, 0)),
            pl.BlockSpec((1, d, tn), lambda l, j: (l, 0, j)),
            pl.BlockSpec((1, 1, tn), lambda l, j: (l, 0, j)),
        ],
        out_specs=pl.BlockSpec((1, MOD_ROWS, tn), lambda l, j: (l, 0, j)),
        out_shape=jax.ShapeDtypeStruct((depth, MOD_ROWS, n), F32),
        compiler_params=_params(2),
        name="modulation",
    )(cc, w_mod, b_mod.reshape(depth, 1, n))


def _inproj_kernel(*refs, tm, seq, rope, d_pool, d_attn, d_kv, d_sg):
    if rope:
        (xp_ref, x_ref, xn_ref, mod_ref, g1_ref, wa_ref, wpool_ref, pscale_ref, sgg_ref, wsp_ref, bsp_ref,
         cos_ref, sin_ref, ypool_ref, q_ref, k4_ref, v4_ref, ysg_ref) = refs
    else:
        (xp_ref, x_ref, xn_ref, mod_ref, g1_ref, wa_ref, wpool_ref, pscale_ref, sgg_ref, wsp_ref, bsp_ref,
         ypool_ref, q_ref, k4_ref, v4_ref, ysg_ref) = refs
    halo = SUBLANES
    n_ext = tm + 2 * halo
    tiles_per_seq = seq // tm
    t_in_seq = lax.rem(pl.program_id(0), tiles_per_seq)
    start = t_in_seq * tm

    shift = mod_ref[0, 0:1, :]
    scale = mod_ref[0, 1:2, :]
    gain = g1_ref[...]
    h_mid = _modnorm(x_ref[...], gain, shift, scale)
    h_ext = jnp.concatenate([_modnorm(xp_ref[...], gain, shift, scale), h_mid,
                             _modnorm(xn_ref[...], gain, shift, scale)], axis=0)
    zp = jnp.dot(h_ext.astype(BF16), wa_ref[:, :d_pool], preferred_element_type=F32)
    zr = jnp.dot(h_mid.astype(BF16), wa_ref[:, d_pool:], preferred_element_type=F32)

    rows = lax.broadcasted_iota(jnp.int32, (n_ext, LANES), 0)
    lo_row = jnp.where(t_in_seq == 0, halo, 0)
    hi_row = jnp.where(t_in_seq == tiles_per_seq - 1, tm + halo, n_ext)
    keep = jnp.logical_and(rows >= lo_row, rows < hi_row)
    pos = rows + (start - halo)

    def shifted(a, s):
        return pltpu.roll(a, s % n_ext, 0)

    for g, w in enumerate(POOL_WINDOWS):
        half = w // 2
        cols = slice(g * LANES, (g + 1) * LANES)
        e = jnp.where(keep, zp[:, cols], 0.0)
        trail = e
        s = 1
        while s < half:
            trail = trail + shifted(trail, s)
            s *= 2
        wsum = shifted(trail, -(half - 1)) + shifted(trail, 1) if half > 1 else trail + shifted(trail, 1)
        cnt = jnp.minimum(pos + half, seq) - jnp.maximum(pos - half, 0)
        inv = 1.0 / cnt.astype(F32)
        pooled = (wsum * inv - e)[halo:halo + tm]
        mixed = jnp.dot(pooled.astype(BF16), wpool_ref[g], preferred_element_type=F32)
        ypool_ref[:, cols] = (mixed * pscale_ref[:, cols]).astype(BF16)

    lane = lax.broadcasted_iota(jnp.int32, (tm, LANES), 1)
    low_half = lane < HEAD_DIM
    o_k = d_attn
    o_v = o_k + d_kv
    o_u = o_v + d_kv
    o_sv = o_u + d_sg
    qscale = HEAD_DIM ** -0.5

    if rope:
        cos = cos_ref[...]
        sin = sin_ref[...]
        first16 = (lane % 32) < 16

        def rot(a):
            swapped = jnp.where(first16, pltpu.roll(a, LANES - 16, 1), pltpu.roll(a, 16, 1))
            return a * cos + swapped * sin
    else:
        def rot(a):
            return a

    for j in range(d_attn // LANES):
        cols = slice(j * LANES, (j + 1) * LANES)
        q_ref[:, cols] = (rot(zr[:, cols]) * qscale).astype(BF16)

    def tile_heads(a):
        other = pltpu.roll(a, HEAD_DIM, 1)
        a0 = jnp.where(low_half, a, other).astype(BF16)
        a1 = jnp.where(low_half, other, a).astype(BF16)
        return jnp.concatenate([a0, a0, a1, a1], axis=1)

    k4_ref[...] = tile_heads(rot(zr[:, o_k:o_k + d_kv]))
    v4_ref[...] = tile_heads(zr[:, o_v:o_v + d_kv])

    gu = jax.nn.gelu(zr[:, o_u:o_u + d_sg], approximate=True)
    gs = jax.nn.gelu(zr[:, o_sv:o_sv + d_sg], approximate=True)
    ms = jnp.mean(gs * gs, axis=-1, keepdims=True)
    vn = ((gs * lax.rsqrt(ms + NORM_EPS)) * sgg_ref[...]).astype(BF16)
    for c in range(tm // CHUNK):
        r = slice(c * CHUNK, (c + 1) * CHUNK)
        for g in range(N_SG_GROUPS):
            cols = slice(g * LANES, (g + 1) * LANES)
            mixed = jnp.dot(wsp_ref[g], vn[r, cols], preferred_element_type=F32) + bsp_ref[:, cols]
            ysg_ref[r, cols] = (gu[r, cols] * mixed).astype(BF16)


def _inproj(x2, mod_l, mod_row_fn, g1, wa, wpool, pscale, sgg, wsp, bsp, rope_tabs, *, tm, seq, dims):
    t, d = x2.shape
    d_pool, d_attn, d_kv, d_sg = dims
    n_tiles = t // tm
    hb = tm // SUBLANES
    n_hblk = t // SUBLANES
    tiles_per_seq = seq // tm
    rope = rope_tabs is not None
    in_specs = [
        pl.BlockSpec((SUBLANES, d), lambda i: (jnp.maximum(i * hb - 1, 0), 0)),
        pl.BlockSpec((tm, d), lambda i: (i, 0)),
        pl.BlockSpec((SUBLANES, d), lambda i: (jnp.minimum((i + 1) * hb, n_hblk - 1), 0)),
        pl.BlockSpec((1, N_MOD, d), lambda i: (mod_row_fn(i), 0, 0)),
        _const_spec(g1.shape), _const_spec(wa.shape), _const_spec(wpool.shape), _const_spec(pscale.shape),
        _const_spec(sgg.shape), _const_spec(wsp.shape), _const_spec(bsp.shape),
    ]
    args = [x2, x2, x2, mod_l, g1, wa, wpool, pscale, sgg, wsp, bsp]
    if rope:
        in_specs += [pl.BlockSpec((tm, LANES), lambda i: (lax.rem(i, tiles_per_seq), 0))] * 2
        args += list(rope_tabs)
    widths = (d_pool, d_attn, 2 * GROUP_LANES, 2 * GROUP_LANES, d_sg)
    return pl.pallas_call(
        functools.partial(_inproj_kernel, tm=tm, seq=seq, rope=rope, d_pool=d_pool, d_attn=d_attn,
                          d_kv=d_kv, d_sg=d_sg),
        grid=(n_tiles,),
        in_specs=in_specs,
        out_specs=[pl.BlockSpec((tm, w), lambda i: (i, 0)) for w in widths],
        out_shape=[jax.ShapeDtypeStruct((t, w), BF16) for w in widths],
        compiler_params=_params(1),
        name="inproj_rope" if rope else "inproj_ctx",
    )(*args)


def _attend(q_blk, key_blocks, val_blocks, masks, sink_col, head_of_lane):
    zero = jnp.zeros_like(q_blk)
    q4 = jnp.concatenate([jnp.where(head_of_lane == j, q_blk, zero) for j in range(4)], axis=0)
    logits = []
    for kb, mk in zip(key_blocks, masks):
        s = lax.dot_general(q4, kb, (((1,), (1,)), ((), ())), preferred_element_type=F32)
        if mk is not None:
            s = jnp.where(mk, s, MASK_VALUE)
        logits.append(s)
    m = sink_col
    for s in logits:
        m = jnp.maximum(m, jnp.max(s, axis=-1, keepdims=True))
    denom = jnp.exp(sink_col - m)
    acc = None
    for s, vb in zip(logits, val_blocks):
        p = jnp.exp(s - m)
        denom = denom + jnp.sum(p, axis=-1, keepdims=True)
        pv = jnp.dot(p.astype(BF16), vb, preferred_element_type=F32)
        acc = pv if acc is None else acc + pv
    r = acc / denom
    out = None
    for j in range(4):
        part = jnp.where(head_of_lane == j, r[j * ATTN_BLOCK:(j + 1) * ATTN_BLOCK], 0.0)
        out = part if out is None else out + part
    return out


def _win_attn_kernel(q_ref, kp_ref, ko_ref, kn_ref, vp_ref, vo_ref, vn_ref, kc_ref, vc_ref, sink_ref, o_ref,
                     *, qt, n_steps):
    j = pl.program_id(1)
    nsub = qt // ATTN_BLOCK
    head_of_lane = lax.broadcasted_iota(jnp.int32, (ATTN_BLOCK, GROUP_LANES), 1) // HEAD_DIM
    row = lax.rem(lax.broadcasted_iota(jnp.int32, (4 * ATTN_BLOCK, ATTN_BLOCK), 0), ATTN_BLOCK)
    col = lax.broadcasted_iota(jnp.int32, (4 * ATTN_BLOCK, ATTN_BLOCK), 1)
    tri_prev = col >= row
    tri_next = col <= row
    edge_prev = col >= row + jnp.where(j > 0, 0, 2 * ATTN_BLOCK)
    edge_next = col <= row - jnp.where(j < n_steps - 1, 0, 2 * ATTN_BLOCK)
    for h in range(2):
        lanes = slice(h * GROUP_LANES, (h + 1) * GROUP_LANES)
        sink_col = sink_ref[h][:, 0:1]
        for n in range(nsub):
            r = slice(n * ATTN_BLOCK, (n + 1) * ATTN_BLOCK)
            if n > 0:
                rp = slice((n - 1) * ATTN_BLOCK, n * ATTN_BLOCK)
                k_prev, v_prev, m_prev = ko_ref[rp, lanes], vo_ref[rp, lanes], tri_prev
            else:
                k_prev, v_prev = kp_ref[:, lanes], vp_ref[:, lanes]
                m_prev = edge_prev
            if n < nsub - 1:
                rn = slice((n + 1) * ATTN_BLOCK, (n + 2) * ATTN_BLOCK)
                k_next, v_next, m_next = ko_ref[rn, lanes], vo_ref[rn, lanes], tri_next
            else:
                k_next, v_next = kn_ref[:, lanes], vn_ref[:, lanes]
                m_next = edge_next
            out = _attend(q_ref[r, lanes],
                          [k_prev, ko_ref[r, lanes], k_next, kc_ref[:, lanes]],
                          [v_prev, vo_ref[r, lanes], v_next, vc_ref[:, lanes]],
                          [m_prev, None, m_next, None], sink_col, head_of_lane)
            o_ref[r, lanes] = out.astype(BF16)


def _window_attention(q, k4, v4, kc4, vc4, sink_tab, *, n_batch, seq, ctx_len, qt):
    t, dq = q.shape
    n_steps = seq // qt
    sub = qt // ATTN_BLOCK
    blocks_per_seq = seq // ATTN_BLOCK
    own = lambda b, j: (b * n_steps + j, 0)
    prev = lambda b, j: (b * blocks_per_seq + jnp.maximum(j * sub - 1, 0), 0)
    nxt = lambda b, j: (b * blocks_per_seq + jnp.minimum((j + 1) * sub, blocks_per_seq - 1), 0)
    ctx = lambda b, j: (b, 0)
    return pl.pallas_call(
        functools.partial(_win_attn_kernel, qt=qt, n_steps=n_steps),
        grid=(n_batch, n_steps),
        in_specs=[
            pl.BlockSpec((qt, dq), own),
            pl.BlockSpec((ATTN_BLOCK, dq), prev), pl.BlockSpec((qt, dq), own), pl.BlockSpec((ATTN_BLOCK, dq), nxt),
            pl.BlockSpec((ATTN_BLOCK, dq), prev), pl.BlockSpec((qt, dq), own), pl.BlockSpec((ATTN_BLOCK, dq), nxt),
            pl.BlockSpec((ctx_len, dq), ctx), pl.BlockSpec((ctx_len, dq), ctx),
            pl.BlockSpec(sink_tab.shape, lambda b, j: (0, 0, 0)),
        ],
        out_specs=pl.BlockSpec((qt, dq), own),
        out_shape=jax.ShapeDtypeStruct((t, dq), BF16),
        compiler_params=_params(2),
        name="window_attention",
    )(q, k4, k4, k4, v4, v4, v4, kc4, vc4, sink_tab)


def _ctx_attn_kernel(q_ref, k_ref, v_ref, sink_ref, o_ref, *, ctx_len):
    head_of_lane = lax.broadcasted_iota(jnp.int32, (ATTN_BLOCK, GROUP_LANES), 1) // HEAD_DIM
    for h in range(2):
        lanes = slice(h * GROUP_LANES, (h + 1) * GROUP_LANES)
        sink_col = sink_ref[h][:, 0:1]
        for n in range(ctx_len // ATTN_BLOCK):
            r = slice(n * ATTN_BLOCK, (n + 1) * ATTN_BLOCK)
            out = _attend(q_ref[r, lanes], [k_ref[:, lanes]], [v_ref[:, lanes]], [None], sink_col, head_of_lane)
            o_ref[r, lanes] = out.astype(BF16)


def _context_attention(q, k4, v4, sink_tab, *, n_batch, ctx_len):
    t, dq = q.shape
    blk = pl.BlockSpec((ctx_len, dq), lambda b: (b, 0))
    return pl.pallas_call(
        functools.partial(_ctx_attn_kernel, ctx_len=ctx_len),
        grid=(n_batch,),
        in_specs=[blk, blk, blk, pl.BlockSpec(sink_tab.shape, lambda b: (0, 0, 0))],
        out_specs=blk,
        out_shape=jax.ShapeDtypeStruct((t, dq), BF16),
        compiler_params=_params(1),
        name="context_attention",
    )(q, k4, v4, sink_tab)


def _merge_kernel(x_ref, mod_ref, g1_ref, yp_ref, ya_ref, ys_ref, wg_ref, wbp_ref, wba_ref, wbs_ref, wo_ref, o_ref):
    x = x_ref[...]
    d = x.shape[-1]
    shift, scale, gate = mod_ref[0, 0:1, :], mod_ref[0, 1:2, :], mod_ref[0, 2:3, :]
    hx = _modnorm(x, g1_ref[...], shift, scale).astype(BF16)
    y = None
    for b, (y_ref, w_ref) in enumerate(((yp_ref, wbp_ref), (ya_ref, wba_ref), (ys_ref, wbs_ref))):
        gate_b = jax.nn.sigmoid(jnp.dot(hx, wg_ref[:, b * d:(b + 1) * d], preferred_element_type=F32))
        part = gate_b * jnp.dot(y_ref[...], w_ref[...], preferred_element_type=F32)
        y = part if y is None else y + part
    o = jnp.dot(y.astype(BF16), wo_ref[...], preferred_element_type=F32)
    o_ref[...] = x + gate * o


def _merge(x2, mod_l, mod_row_fn, g1, yp, ya, ys, wg, wbp, wba, wbs, wo, *, tm):
    t, d = x2.shape
    row = lambda i: (i, 0)
    return pl.pallas_call(
        _merge_kernel,
        grid=(t // tm,),
        in_specs=[
            pl.BlockSpec((tm, d), row),
            pl.BlockSpec((1, N_MOD, d), lambda i: (mod_row_fn(i), 0, 0)),
            _const_spec(g1.shape),
            pl.BlockSpec((tm, yp.shape[1]), row), pl.BlockSpec((tm, ya.shape[1]), row),
            pl.BlockSpec((tm, ys.shape[1]), row),
            _const_spec(wg.shape), _const_spec(wbp.shape), _const_spec(wba.shape), _const_spec(wbs.shape),
            _const_spec(wo.shape),
        ],
        out_specs=pl.BlockSpec((tm, d), row),
        out_shape=jax.ShapeDtypeStruct((t, d), F32),
        compiler_params=_params(1),
        name="merge",
    )(x2, mod_l, g1, yp, ya, ys, wg, wbp, wba, wbs, wo)


def _ffn_chunks(d_ff):
    n_tiles = d_ff // MXU_DIM
    first = (n_tiles + 1) // 2 * MXU_DIM
    return ((0, first), (first, d_ff))


def _ffn_kernel(*refs, d_ff, final):
    if final:
        x_ref, mod_ref, g2_ref, wi_ref, wo_ref, fg_ref, o_ref = refs
    else:
        x_ref, mod_ref, g2_ref, wi_ref, wo_ref, o_ref = refs
    x = x_ref[...]
    shift, scale, gate = mod_ref[0, 3:4, :], mod_ref[0, 4:5, :], mod_ref[0, 5:6, :]
    h = _modnorm(x, g2_ref[...], shift, scale).astype(BF16)
    acc = None
    for lo, hi in _ffn_chunks(d_ff):
        a = jnp.dot(h, wi_ref[:, lo:hi], preferred_element_type=F32)
        b = jnp.dot(h, wi_ref[:, d_ff + lo:d_ff + hi], preferred_element_type=F32)
        act = ((a * jax.nn.sigmoid(a)) * b).astype(BF16)
        part = jnp.dot(act, wo_ref[lo:hi, :], preferred_element_type=F32)
        acc = part if acc is None else acc + part
    y = x + gate * acc
    if final:
        ms = jnp.mean(y * y, axis=-1, keepdims=True)
        y = (y * lax.rsqrt(ms + NORM_EPS)) * fg_ref[...]
    o_ref[...] = y


def _ffn(x2, mod_l, mod_row_fn, g2, wi, wo, final_gain, *, tm):
    t, d = x2.shape
    d_ff = wo.shape[0]
    final = final_gain is not None
    row = lambda i: (i, 0)
    in_specs = [
        pl.BlockSpec((tm, d), row),
        pl.BlockSpec((1, N_MOD, d), lambda i: (mod_row_fn(i), 0, 0)),
        _const_spec(g2.shape), _const_spec(wi.shape), _const_spec(wo.shape),
    ]
    args = [x2, mod_l, g2, wi, wo]
    if final:
        in_specs.append(_const_spec(final_gain.shape))
        args.append(final_gain)
    return pl.pallas_call(
        functools.partial(_ffn_kernel, d_ff=d_ff, final=final),
        grid=(t // tm,),
        in_specs=in_specs,
        out_specs=pl.BlockSpec((tm, d), row),
        out_shape=jax.ShapeDtypeStruct((t, d), F32),
        compiler_params=_params(1),
        name="ffn_final" if final else "ffn",
    )(*args)


def _rope_tables(seq):
    rows = seq // GRID_W
    freqs = HEAD_DIM // 4
    row = jnp.repeat(jnp.arange(rows), GRID_W).astype(F32)
    col = jnp.tile(jnp.arange(GRID_W), rows).astype(F32)
    inv_freq = ROPE_THETA ** (-jnp.arange(freqs, dtype=F32) / freqs)
    ang_r = row[:, None] * inv_freq[None, :]
    ang_c = col[:, None] * inv_freq[None, :]
    cos = jnp.concatenate([jnp.cos(ang_r), jnp.cos(ang_r), jnp.cos(ang_c), jnp.cos(ang_c)], axis=-1)
    sin = jnp.concatenate([-jnp.sin(ang_r), jnp.sin(ang_r), -jnp.sin(ang_c), jnp.sin(ang_c)], axis=-1)
    reps = LANES // HEAD_DIM
    return jnp.tile(cos, (1, reps)), jnp.tile(sin, (1, reps))


def _pick_tile(seq, target):
    tm = min(seq, target)
    assert seq % tm == 0 and tm % CHUNK == 0
    return tm


def kernel(x, c, ctx, c_ctx, w_mod, b_mod, norm1_gain, norm2_gain, w_in, w_pool, pool_scale, attn_sink, sg_v_gain,
           w_spatial, b_spatial, w_br_pool, w_br_attn, w_br_sg, w_out, w_ffn_in, w_ffn_out, final_gain):
    n_batch, seq, d = x.shape
    ctx_len = ctx.shape[1]
    depth = w_mod.shape[0]
    d_pool = w_br_pool.shape[1]
    d_attn = w_br_attn.shape[1]
    d_sg = w_br_sg.shape[1]
    n_q_heads = attn_sink.shape[1]
    d_kv = (n_q_heads // 4) * HEAD_DIM
    off_gate = d_pool + d_attn + 2 * d_kv + 2 * d_sg
    assert n_q_heads == 8 and d_pool == len(POOL_WINDOWS) * LANES and d_sg == N_SG_GROUPS * LANES
    assert w_in.shape[2] == off_gate + N_BRANCHES * d and n_batch + 1 <= MOD_ROWS
    assert seq % GRID_W == 0 and w_spatial.shape[-1] == CHUNK
    dims = (d_pool, d_attn, d_kv, d_sg)

    tm_x = _pick_tile(seq, 512)
    tm_c = _pick_tile(ctx_len, 512)
    qt = _pick_tile(seq, 512)

    cc = jnp.concatenate([c, c_ctx[None, :], jnp.zeros((MOD_ROWS - n_batch - 1, d), F32)], axis=0)
    mod = _modulation(cc, w_mod, b_mod).reshape(depth, MOD_ROWS, N_MOD, d)

    rope_tabs = _rope_tables(seq)
    x_tiles_per_seq = seq // tm_x
    mod_row_x = lambda i: i // x_tiles_per_seq
    mod_row_c = lambda i: n_batch

    xs = x.reshape(n_batch * seq, d)
    cs = ctx.reshape(n_batch * ctx_len, d)
    row2 = lambda a: a.reshape(1, -1)

    for l in range(depth):
        last = l == depth - 1
        mod_l = mod[l]
        g1, g2 = row2(norm1_gain[l]), row2(norm2_gain[l])
        wa = w_in[l, :, :off_gate].astype(BF16)
        wg = w_in[l, :, off_gate:].astype(BF16)
        wpool = w_pool[l].astype(BF16)
        wsp = w_spatial[l].astype(BF16)
        pscale, sgg = row2(pool_scale[l]), row2(sg_v_gain[l])
        bsp = jnp.repeat(b_spatial[l].T, LANES, axis=1)
        sink_tab = jnp.broadcast_to(
            jnp.repeat(attn_sink[l].reshape(2, 4), ATTN_BLOCK, axis=1)[:, :, None], (2, 4 * ATTN_BLOCK, LANES))
        wbp, wba, wbs = (w_br_pool[l].astype(BF16), w_br_attn[l].astype(BF16), w_br_sg[l].astype(BF16))
        wo = w_out[l].astype(BF16)
        wfi, wfo = w_ffn_in[l].astype(BF16), w_ffn_out[l].astype(BF16)

        common = (g1, wa, wpool, pscale, sgg, wsp, bsp)
        ypc, qc, kc4, vc4, ysc = _inproj(cs, mod_l, mod_row_c, *common, None, tm=tm_c, seq=ctx_len, dims=dims)
        ypx, qx, kx4, vx4, ysx = _inproj(xs, mod_l, mod_row_x, *common, rope_tabs, tm=tm_x, seq=seq, dims=dims)
        ax = _window_attention(qx, kx4, vx4, kc4, vc4, sink_tab, n_batch=n_batch, seq=seq, ctx_len=ctx_len, qt=qt)
        xs = _merge(xs, mod_l, mod_row_x, g1, ypx, ax, ysx, wg, wbp, wba, wbs, wo, tm=tm_x)
        xs = _ffn(xs, mod_l, mod_row_x, g2, wfi, wfo, row2(final_gain) if last else None, tm=tm_x)
        if not last:
            ac = _context_attention(qc, kc4, vc4, sink_tab, n_batch=n_batch, ctx_len=ctx_len)
            cs = _merge(cs, mod_l, mod_row_c, g1, ypc, ac, ysc, wg, wbp, wba, wbs, wo, tm=tm_c)
            cs = _ffn(cs, mod_l, mod_row_c, g2, wfi, wfo, None, tm=tm_c)
    return xs.reshape(n_batch, seq, d)
```

```python
import functools
import math

import jax
import jax.numpy as jnp
from jax import lax
from jax.experimental import pallas as pl
from jax.experimental.pallas import tpu as pltpu

F32 = jnp.float32
BF16 = jnp.bfloat16

GRID_W = 64
NORM_EPS = 1e-6
MASK_VALUE = -1e30
N_MOD = 6
POOL_WINDOWS = (2, 4, 8, 16)
HEAD_DIM = 64
GQA_GROUP = 4
ATTN_BLOCK = 128
ROPE_THETA = 10000.0
CHUNK = 128
N_SG_GROUPS = 4
N_BRANCHES = 3
LOG2_E = math.log2(math.e)

LANES = 128
SUBLANES = 8
MXU_DIM = 256
VMEM_LIMIT_BYTES = 56 * 1024 * 1024

MOD_ROWS = 16
QCOLS = GQA_GROUP * ATTN_BLOCK

NT_DIMS = (((1,), (1,)), ((), ()))
TN_DIMS = (((0,), (0,)), ((), ()))


def _const_spec(shape):
    nd = len(shape)
    return pl.BlockSpec(shape, lambda *_: (0,) * nd, pipeline_mode=pl.Buffered(1))


def _params(n_axes=1):
    return pltpu.CompilerParams(dimension_semantics=("arbitrary",) * n_axes,
                                vmem_limit_bytes=VMEM_LIMIT_BYTES)


def _modnorm(xf, gain, shift, scale):
    ms = jnp.mean(xf * xf, axis=-1, keepdims=True)
    y = xf * lax.rsqrt(ms + NORM_EPS)
    return (y * gain) * (1.0 + scale) + shift


def _mod_kernel(c_ref, w_ref, b_ref, o_ref):
    c = c_ref[...]
    s = (c * jax.nn.sigmoid(c)).astype(BF16)
    o_ref[0] = jnp.dot(s, w_ref[0].astype(BF16), preferred_element_type=F32) + b_ref[0]


def _modulation(cc, w_mod, b_mod):
    depth, d, n = w_mod.shape
    tn = 1024
    return pl.pallas_call(
        _mod_kernel,
        grid=(depth, n // tn),
        in_specs=[
            pl.BlockSpec((MOD_ROWS, d), lambda l, j: (0, 0)),
            pl.BlockSpec((1, d, tn), lambda l, j: (l, 0, j)),
            pl.BlockSpec((1, 1, tn), lambda l, j: (l, 0, j)),
        ],
        out_specs=pl.BlockSpec((1, MOD_ROWS, tn), lambda l, j: (l, 0, j)),
        out_shape=jax.ShapeDtypeStruct((depth, MOD_ROWS, n), F32),
        compiler_params=_params(2),
        name="modulation",
    )(cc, w_mod, b_mod.reshape(depth, 1, n))


def _inproj_kernel(*refs, tm, seq, rope, d_pool, d_attn, d_kv, d_sg):
    if rope:
        (xp_ref, x_ref, xn_ref, mod_ref, g1_ref, wa_ref, wqv_ref, wpool_ref, pscale_ref, sgg_ref, wsp_ref, bsp_ref,
         cos_ref, sin_ref, cos_t_ref, sin_t_ref, ypool_ref, qt_ref, k_ref, vt_ref, ysg_ref) = refs
    else:
        (xp_ref, x_ref, xn_ref, mod_ref, g1_ref, wa_ref, wqv_ref, wpool_ref, pscale_ref, sgg_ref, wsp_ref, bsp_ref,
         ypool_ref, qt_ref, k_ref, vt_ref, ysg_ref) = refs
    halo = SUBLANES
    n_ext = tm + 2 * halo
    tiles_per_seq = seq // tm
    t_in_seq = lax.rem(pl.program_id(0), tiles_per_seq)
    start = t_in_seq * tm

    shift = mod_ref[0, 0:1, :]
    scale = mod_ref[0, 1:2, :]
    gain = g1_ref[...]
    h_mid = _modnorm(x_ref[...], gain, shift, scale)
    h_ext = jnp.concatenate([_modnorm(xp_ref[...], gain, shift, scale), h_mid,
                             _modnorm(xn_ref[...], gain, shift, scale)], axis=0)
    h_bf = h_mid.astype(BF16)
    zp = jnp.dot(h_ext.astype(BF16), wa_ref[:, :d_pool], preferred_element_type=F32)
    zr = jnp.dot(h_bf, wa_ref[:, d_pool:], preferred_element_type=F32)
    zt = lax.dot_general(wqv_ref[...], h_bf, NT_DIMS, preferred_element_type=F32)

    rows = lax.broadcasted_iota(jnp.int32, (n_ext, LANES), 0)
    lo_row = jnp.where(t_in_seq == 0, halo, 0)
    hi_row = jnp.where(t_in_seq == tiles_per_seq - 1, tm + halo, n_ext)
    keep = jnp.logical_and(rows >= lo_row, rows < hi_row)
    pos = rows + (start - halo)

    def shifted(a, s):
        return pltpu.roll(a, s % n_ext, 0)

    for g, w in enumerate(POOL_WINDOWS):
        half = w // 2
        cols = slice(g * LANES, (g + 1) * LANES)
        e = jnp.where(keep, zp[:, cols], 0.0)
        trail = e
        s = 1
        while s < half:
            trail = trail + shifted(trail, s)
            s *= 2
        wsum = shifted(trail, -(half - 1)) + shifted(trail, 1) if half > 1 else trail + shifted(trail, 1)
        cnt = jnp.minimum(pos + half, seq) - jnp.maximum(pos - half, 0)
        inv = 1.0 / cnt.astype(F32)
        pooled = (wsum * inv - e)[halo:halo + tm]
        mixed = jnp.dot(pooled.astype(BF16), wpool_ref[g], preferred_element_type=F32)
        ypool_ref[:, cols] = (mixed * pscale_ref[:, cols]).astype(BF16)

    qscale = HEAD_DIM ** -0.5 * LOG2_E
    quarter = HEAD_DIM // 4
    for hq in range(d_attn // HEAD_DIM):
        blk = zt[hq * HEAD_DIM:(hq + 1) * HEAD_DIM]
        if rope:
            swapped = jnp.concatenate([blk[quarter:2 * quarter], blk[:quarter],
                                       blk[3 * quarter:], blk[2 * quarter:3 * quarter]], axis=0)
            blk = blk * cos_t_ref[...] + swapped * sin_t_ref[...]
        qt_ref[hq * HEAD_DIM:(hq + 1) * HEAD_DIM, :] = (blk * qscale).astype(BF16)
    vt_ref[...] = zt[d_attn:d_attn + d_kv].astype(BF16)

    zk = zr[:, :d_kv]
    if rope:
        lane = lax.broadcasted_iota(jnp.int32, (tm, LANES), 1)
        first16 = (lane % (2 * quarter)) < quarter
        swapped = jnp.where(first16, pltpu.roll(zk, LANES - quarter, 1), pltpu.roll(zk, quarter, 1))
        zk = zk * cos_ref[...] + swapped * sin_ref[...]
    k_ref[...] = zk.astype(BF16)

    o_u = d_kv
    o_sv = o_u + d_sg
    gu = jax.nn.gelu(zr[:, o_u:o_u + d_sg], approximate=True)
    gs = jax.nn.gelu(zr[:, o_sv:o_sv + d_sg], approximate=True)
    ms = jnp.mean(gs * gs, axis=-1, keepdims=True)
    vn = ((gs * lax.rsqrt(ms + NORM_EPS)) * sgg_ref[...]).astype(BF16)
    for c in range(tm // CHUNK):
        r = slice(c * CHUNK, (c + 1) * CHUNK)
        for g in range(N_SG_GROUPS):
            cols = slice(g * LANES, (g + 1) * LANES)
            mixed = jnp.dot(wsp_ref[g], vn[r, cols], preferred_element_type=F32) + bsp_ref[:, cols]
            ysg_ref[r, cols] = (gu[r, cols] * mixed).astype(BF16)


def _inproj(x2, mod_l, mod_row_fn, g1, wa, wqv, wpool, pscale, sgg, wsp, bsp, rope_tabs, *, tm, seq, dims):
    t, d = x2.shape
    d_pool, d_attn, d_kv, d_sg = dims
    n_tiles = t // tm
    hb = tm // SUBLANES
    n_hblk = t // SUBLANES
    tiles_per_seq = seq // tm
    rope = rope_tabs is not None
    in_specs = [
        pl.BlockSpec((SUBLANES, d), lambda i: (jnp.maximum(i * hb - 1, 0), 0)),
        pl.BlockSpec((tm, d), lambda i: (i, 0)),
        pl.BlockSpec((SUBLANES, d), lambda i: (jnp.minimum((i + 1) * hb, n_hblk - 1), 0)),
        pl.BlockSpec((1, N_MOD, d), lambda i: (mod_row_fn(i), 0, 0)),
        _const_spec(g1.shape), _const_spec(wa.shape), _const_spec(wqv.shape), _const_spec(wpool.shape),
        _const_spec(pscale.shape), _const_spec(sgg.shape), _const_spec(wsp.shape), _const_spec(bsp.shape),
    ]
    args = [x2, x2, x2, mod_l, g1, wa, wqv, wpool, pscale, sgg, wsp, bsp]
    if rope:
        in_specs += [pl.BlockSpec((tm, LANES), lambda i: (lax.rem(i, tiles_per_seq), 0))] * 2
        in_specs += [pl.BlockSpec((HEAD_DIM, tm), lambda i: (0, lax.rem(i, tiles_per_seq)))] * 2
        args += list(rope_tabs)
    row = lambda i: (i, 0)
    col = lambda i: (0, i)
    return pl.pallas_call(
        functools.partial(_inproj_kernel, tm=tm, seq=seq, rope=rope, d_pool=d_pool, d_attn=d_attn,
                          d_kv=d_kv, d_sg=d_sg),
        grid=(n_tiles,),
        in_specs=in_specs,
        out_specs=[pl.BlockSpec((tm, d_pool), row), pl.BlockSpec((d_attn, tm), col), pl.BlockSpec((tm, d_kv), row),
                   pl.BlockSpec((d_kv, tm), col), pl.BlockSpec((tm, d_sg), row)],
        out_shape=[jax.ShapeDtypeStruct((t, d_pool), BF16), jax.ShapeDtypeStruct((d_attn, t), BF16),
                   jax.ShapeDtypeStruct((t, d_kv), BF16), jax.ShapeDtypeStruct((d_kv, t), BF16),
                   jax.ShapeDtypeStruct((t, d_sg), BF16)],
        compiler_params=_params(1),
        name="inproj_rope" if rope else "inproj_ctx",
    )(*args)


def _attend_t(q_rhs, k_cat, vt_cat, masks, sink_row):
    s = jnp.dot(k_cat, q_rhs, preferred_element_type=F32)
    if masks:
        pieces, at = [], 0
        for rs, mk in masks:
            if rs.start > at:
                pieces.append(s[at:rs.start])
            pieces.append(jnp.where(mk, s[rs], MASK_VALUE))
            at = rs.stop
        if at < s.shape[0]:
            pieces.append(s[at:])
        s = jnp.concatenate(pieces, axis=0)
    m = jnp.maximum(jnp.max(s, axis=0, keepdims=True), sink_row)
    p = jnp.exp2(s - m)
    denom = jnp.sum(p, axis=0, keepdims=True) + jnp.exp2(sink_row - m)
    o = jnp.dot(vt_cat, p.astype(BF16), preferred_element_type=F32)
    return o / denom


def _q_rhs(qt_ref, h, cols):
    blocks = [qt_ref[(GQA_GROUP * h + j) * HEAD_DIM:(GQA_GROUP * h + j + 1) * HEAD_DIM, cols]
              for j in range(GQA_GROUP)]
    q = jnp.concatenate(blocks, axis=1)
    z = jnp.zeros_like(q)
    return jnp.concatenate([q, z] if h == 0 else [z, q], axis=0)


def _store_heads(o_ref, h, cols, out):
    for j in range(GQA_GROUP):
        hq = GQA_GROUP * h + j
        o_ref[hq * HEAD_DIM:(hq + 1) * HEAD_DIM, cols] = out[:, j * ATTN_BLOCK:(j + 1) * ATTN_BLOCK].astype(BF16)


def _win_attn_kernel(qt_ref, kp_ref, ko_ref, kn_ref, vp_ref, vo_ref, vn_ref, kc_ref, vc_ref, sink_ref, o_ref,
                     *, qt, n_steps):
    step = pl.program_id(1)
    nsub = qt // ATTN_BLOCK
    key = lax.broadcasted_iota(jnp.int32, (ATTN_BLOCK, QCOLS), 0)
    qry = lax.rem(lax.broadcasted_iota(jnp.int32, (ATTN_BLOCK, QCOLS), 1), ATTN_BLOCK)
    tri_prev = key >= qry
    tri_next = key <= qry
    edge_prev = key >= qry + jnp.where(step > 0, 0, 2 * ATTN_BLOCK)
    edge_next = key <= qry - jnp.where(step < n_steps - 1, 0, 2 * ATTN_BLOCK)
    prev_rows = slice(0, ATTN_BLOCK)
    next_rows = slice(2 * ATTN_BLOCK, 3 * ATTN_BLOCK)
    for h in range(2):
        feat = slice(h * HEAD_DIM, (h + 1) * HEAD_DIM)
        sink_row = sink_ref[h] * LOG2_E
        for n in range(nsub):
            r = slice(n * ATTN_BLOCK, (n + 1) * ATTN_BLOCK)
            if n > 0:
                rp = slice((n - 1) * ATTN_BLOCK, n * ATTN_BLOCK)
                k_prev, v_prev, m_prev = ko_ref[rp, :], vo_ref[feat, rp], tri_prev
            else:
                k_prev, v_prev, m_prev = kp_ref[...], vp_ref[feat, :], edge_prev
            if n < nsub - 1:
                rn = slice((n + 1) * ATTN_BLOCK, (n + 2) * ATTN_BLOCK)
                k_next, v_next, m_next = ko_ref[rn, :], vo_ref[feat, rn], tri_next
            else:
                k_next, v_next, m_next = kn_ref[...], vn_ref[feat, :], edge_next
            k_cat = jnp.concatenate([k_prev, ko_ref[r, :], k_next, kc_ref[...]], axis=0)
            vt_cat = jnp.concatenate([v_prev, vo_ref[feat, r], v_next, vc_ref[feat, :]], axis=1)
            out = _attend_t(_q_rhs(qt_ref, h, r), k_cat, vt_cat,
                            [(prev_rows, m_prev), (next_rows, m_next)], sink_row)
            _store_heads(o_ref, h, r, out)


def _window_attention(q_t, k, v_t, kc, vc_t, sink_tab, *, n_batch, seq, ctx_len, qt):
    dq, t = q_t.shape
    dkv = k.shape[1]
    n_steps = seq // qt
    sub = qt // ATTN_BLOCK
    blocks_per_seq = seq // ATTN_BLOCK
    own = lambda b, j: b * n_steps + j
    prev = lambda b, j: b * blocks_per_seq + jnp.maximum(j * sub - 1, 0)
    nxt = lambda b, j: b * blocks_per_seq + jnp.minimum((j + 1) * sub, blocks_per_seq - 1)
    rows = lambda f: (lambda b, j: (f(b, j), 0))
    cols = lambda f: (lambda b, j: (0, f(b, j)))
    return pl.pallas_call(
        functools.partial(_win_attn_kernel, qt=qt, n_steps=n_steps),
        grid=(n_batch, n_steps),
        in_specs=[
            pl.BlockSpec((dq, qt), cols(own)),
            pl.BlockSpec((ATTN_BLOCK, dkv), rows(prev)), pl.BlockSpec((qt, dkv), rows(own)),
            pl.BlockSpec((ATTN_BLOCK, dkv), rows(nxt)),
            pl.BlockSpec((dkv, ATTN_BLOCK), cols(prev)), pl.BlockSpec((dkv, qt), cols(own)),
            pl.BlockSpec((dkv, ATTN_BLOCK), cols(nxt)),
            pl.BlockSpec((ctx_len, dkv), lambda b, j: (b, 0)), pl.BlockSpec((dkv, ctx_len), lambda b, j: (0, b)),
            pl.BlockSpec(sink_tab.shape, lambda b, j: (0, 0, 0)),
        ],
        out_specs=pl.BlockSpec((dq, qt), cols(own)),
        out_shape=jax.ShapeDtypeStruct((dq, t), BF16),
        compiler_params=_params(2),
        name="window_attention",
    )(q_t, k, k, k, v_t, v_t, v_t, kc, vc_t, sink_tab)


def _ctx_attn_kernel(qt_ref, k_ref, vt_ref, sink_ref, o_ref, *, ctx_len):
    for h in range(2):
        feat = slice(h * HEAD_DIM, (h + 1) * HEAD_DIM)
        sink_row = sink_ref[h] * LOG2_E
        for n in range(ctx_len // ATTN_BLOCK):
            r = slice(n * ATTN_BLOCK, (n + 1) * ATTN_BLOCK)
            out = _attend_t(_q_rhs(qt_ref, h, r), k_ref[...], vt_ref[feat, :], [], sink_row)
            _store_heads(o_ref, h, r, out)


def _context_attention(q_t, k, v_t, sink_tab, *, n_batch, ctx_len):
    dq, t = q_t.shape
    dkv = k.shape[1]
    return pl.pallas_call(
        functools.partial(_ctx_attn_kernel, ctx_len=ctx_len),
        grid=(n_batch,),
        in_specs=[pl.BlockSpec((dq, ctx_len), lambda b: (0, b)), pl.BlockSpec((ctx_len, dkv), lambda b: (b, 0)),
                  pl.BlockSpec((dkv, ctx_len), lambda b: (0, b)), pl.BlockSpec(sink_tab.shape, lambda b: (0, 0, 0))],
        out_specs=pl.BlockSpec((dq, ctx_len), lambda b: (0, b)),
        out_shape=jax.ShapeDtypeStruct((dq, t), BF16),
        compiler_params=_params(1),
        name="context_attention",
    )(q_t, k, v_t, sink_tab)


def _merge_kernel(x_ref, mod_ref, g1_ref, yp_ref, yat_ref, ys_ref, wg_ref, wbp_ref, wba_ref, wbs_ref, wo_ref, o_ref):
    x = x_ref[...]
    d = x.shape[-1]
    shift, scale, gate = mod_ref[0, 0:1, :], mod_ref[0, 1:2, :], mod_ref[0, 2:3, :]
    hx = _modnorm(x, g1_ref[...], shift, scale).astype(BF16)
    branches = (
        jnp.dot(yp_ref[...], wbp_ref[...], preferred_element_type=F32),
        lax.dot_general(yat_ref[...], wba_ref[...], TN_DIMS, preferred_element_type=F32),
        jnp.dot(ys_ref[...], wbs_ref[...], preferred_element_type=F32),
    )
    y = None
    for b, proj in enumerate(branches):
        gate_b = jax.nn.sigmoid(jnp.dot(hx, wg_ref[:, b * d:(b + 1) * d], preferred_element_type=F32))
        y = gate_b * proj if y is None else y + gate_b * proj
    o = jnp.dot(y.astype(BF16), wo_ref[...], preferred_element_type=F32)
    o_ref[...] = x + gate * o


def _merge(x2, mod_l, mod_row_fn, g1, yp, ya_t, ys, wg, wbp, wba, wbs, wo, *, tm):
    t, d = x2.shape
    row = lambda i: (i, 0)
    return pl.pallas_call(
        _merge_kernel,
        grid=(t // tm,),
        in_specs=[
            pl.BlockSpec((tm, d), row),
            pl.BlockSpec((1, N_MOD, d), lambda i: (mod_row_fn(i), 0, 0)),
            _const_spec(g1.shape),
            pl.BlockSpec((tm, yp.shape[1]), row), pl.BlockSpec((ya_t.shape[0], tm), lambda i: (0, i)),
            pl.BlockSpec((tm, ys.shape[1]), row),
            _const_spec(wg.shape), _const_spec(wbp.shape), _const_spec(wba.shape), _const_spec(wbs.shape),
            _const_spec(wo.shape),
        ],
        out_specs=pl.BlockSpec((tm, d), row),
        out_shape=jax.ShapeDtypeStruct((t, d), F32),
        compiler_params=_params(1),
        name="merge",
    )(x2, mod_l, g1, yp, ya_t, ys, wg, wbp, wba, wbs, wo)


def _ffn_chunks(d_ff):
    n_tiles = d_ff // MXU_DIM
    first = (n_tiles + 1) // 2 * MXU_DIM
    return ((0, first), (first, d_ff))


def _ffn_kernel(*refs, d_ff, final):
    if final:
        x_ref, mod_ref, g2_ref, wi_ref, wo_ref, fg_ref, o_ref = refs
    else:
        x_ref, mod_ref, g2_ref, wi_ref, wo_ref, o_ref = refs
    x = x_ref[...]
    shift, scale, gate = mod_ref[0, 3:4, :], mod_ref[0, 4:5, :], mod_ref[0, 5:6, :]
    h = _modnorm(x, g2_ref[...], shift, scale).astype(BF16)
    acc = None
    for lo, hi in _ffn_chunks(d_ff):
        a = jnp.dot(h, wi_ref[:, lo:hi], preferred_element_type=F32)
        b = jnp.dot(h, wi_ref[:, d_ff + lo:d_ff + hi], preferred_element_type=F32)
        act = ((a * jax.nn.sigmoid(a)) * b).astype(BF16)
        part = jnp.dot(act, wo_ref[lo:hi, :], preferred_element_type=F32)
        acc = part if acc is None else acc + part
    y = x + gate * acc
    if final:
        ms = jnp.mean(y * y, axis=-1, keepdims=True)
        y = (y * lax.rsqrt(ms + NORM_EPS)) * fg_ref[...]
    o_ref[...] = y


def _ffn(x2, mod_l, mod_row_fn, g2, wi, wo, final_gain, *, tm):
    t, d = x2.shape
    d_ff = wo.shape[0]
    final = final_gain is not None
    row = lambda i: (i, 0)
    in_specs = [
        pl.BlockSpec((tm, d), row),
        pl.BlockSpec((1, N_MOD, d), lambda i: (mod_row_fn(i), 0, 0)),
        _const_spec(g2.shape), _const_spec(wi.shape), _const_spec(wo.shape),
    ]
    args = [x2, mod_l, g2, wi, wo]
    if final:
        in_specs.append(_const_spec(final_gain.shape))
        args.append(final_gain)
    return pl.pallas_call(
        functools.partial(_ffn_kernel, d_ff=d_ff, final=final),
        grid=(t // tm,),
        in_specs=in_specs,
        out_specs=pl.BlockSpec((tm, d), row),
        out_shape=jax.ShapeDtypeStruct((t, d), F32),
        compiler_params=_params(1),
        name="ffn_final" if final else "ffn",
    )(*args)


def _rope_tables(seq):
    rows = seq // GRID_W
    freqs = HEAD_DIM // 4
    row = jnp.repeat(jnp.arange(rows), GRID_W).astype(F32)
    col = jnp.tile(jnp.arange(GRID_W), rows).astype(F32)
    inv_freq = ROPE_THETA ** (-jnp.arange(freqs, dtype=F32) / freqs)
    ang_r = row[:, None] * inv_freq[None, :]
    ang_c = col[:, None] * inv_freq[None, :]
    cos = jnp.concatenate([jnp.cos(ang_r), jnp.cos(ang_r), jnp.cos(ang_c), jnp.cos(ang_c)], axis=-1)
    sin = jnp.concatenate([-jnp.sin(ang_r), jnp.sin(ang_r), -jnp.sin(ang_c), jnp.sin(ang_c)], axis=-1)
    reps = LANES // HEAD_DIM
    return jnp.tile(cos, (1, reps)), jnp.tile(sin, (1, reps)), cos.T, sin.T


def _pick_tile(seq, target):
    tm = min(seq, target)
    assert seq % tm == 0 and tm % CHUNK == 0
    return tm


def kernel(x, c, ctx, c_ctx, w_mod, b_mod, norm1_gain, norm2_gain, w_in, w_pool, pool_scale, attn_sink, sg_v_gain,
           w_spatial, b_spatial, w_br_pool, w_br_attn, w_br_sg, w_out, w_ffn_in, w_ffn_out, final_gain):
    n_batch, seq, d = x.shape
    ctx_len = ctx.shape[1]
    depth = w_mod.shape[0]
    d_pool = w_br_pool.shape[1]
    d_attn = w_br_attn.shape[1]
    d_sg = w_br_sg.shape[1]
    n_q_heads = attn_sink.shape[1]
    d_kv = (n_q_heads // GQA_GROUP) * HEAD_DIM
    off_k = d_pool + d_attn
    off_v = off_k + d_kv
    off_u = off_v + d_kv
    off_gate = off_u + 2 * d_sg
    assert n_q_heads == 2 * GQA_GROUP and d_kv == LANES
    assert d_pool == len(POOL_WINDOWS) * LANES and d_sg == N_SG_GROUPS * LANES
    assert w_in.shape[2] == off_gate + N_BRANCHES * d and n_batch + 1 <= MOD_ROWS
    assert seq % GRID_W == 0 and w_spatial.shape[-1] == CHUNK
    dims = (d_pool, d_attn, d_kv, d_sg)

    tm_x = _pick_tile(seq, 512)
    tm_c = _pick_tile(ctx_len, 512)
    qt = _pick_tile(seq, 512)

    cc = jnp.concatenate([c, c_ctx[None, :], jnp.zeros((MOD_ROWS - n_batch - 1, d), F32)], axis=0)
    mod = _modulation(cc, w_mod, b_mod).reshape(depth, MOD_ROWS, N_MOD, d)

    rope_tabs = _rope_tables(seq)
    x_tiles_per_seq = seq // tm_x
    mod_row_x = lambda i: i // x_tiles_per_seq
    mod_row_c = lambda i: n_batch

    xs = x.reshape(n_batch * seq, d)
    cs = ctx.reshape(n_batch * ctx_len, d)
    row2 = lambda a: a.reshape(1, -1)

    for l in range(depth):
        last = l == depth - 1
        mod_l = mod[l]
        g1, g2 = row2(norm1_gain[l]), row2(norm2_gain[l])
        w_l = w_in[l]
        wa = jnp.concatenate([w_l[:, :d_pool], w_l[:, off_k:off_v], w_l[:, off_u:off_gate]], axis=1).astype(BF16)
        wqv = jnp.concatenate([w_l[:, d_pool:off_k], w_l[:, off_v:off_u]], axis=1).T.astype(BF16)
        wg = w_l[:, off_gate:].astype(BF16)
        wpool = w_pool[l].astype(BF16)
        wsp = w_spatial[l].astype(BF16)
        pscale, sgg = row2(pool_scale[l]), row2(sg_v_gain[l])
        bsp = jnp.repeat(b_spatial[l].T, LANES, axis=1)
        sink_tab = jnp.repeat(attn_sink[l].reshape(2, GQA_GROUP), ATTN_BLOCK, axis=1)[:, None, :]
        wbp, wba, wbs = (w_br_pool[l].astype(BF16), w_br_attn[l].astype(BF16), w_br_sg[l].astype(BF16))
        wo = w_out[l].astype(BF16)
        wfi, wfo = w_ffn_in[l].astype(BF16), w_ffn_out[l].astype(BF16)

        common = (g1, wa, wqv, wpool, pscale, sgg, wsp, bsp)
        ypc, qc_t, kc, vc_t, ysc = _inproj(cs, mod_l, mod_row_c, *common, None, tm=tm_c, seq=ctx_len, dims=dims)
        ypx, qx_t, kx, vx_t, ysx = _inproj(xs, mod_l, mod_row_x, *common, rope_tabs, tm=tm_x, seq=seq, dims=dims)
        ax_t = _window_attention(qx_t, kx, vx_t, kc, vc_t, sink_tab, n_batch=n_batch, seq=seq, ctx_len=ctx_len, qt=qt)
        xs = _merge(xs, mod_l, mod_row_x, g1, ypx, ax_t, ysx, wg, wbp, wba, wbs, wo, tm=tm_x)
        xs = _ffn(xs, mod_l, mod_row_x, g2, wfi, wfo, row2(final_gain) if last else None, tm=tm_x)
        if not last:
            ac_t = _context_attention(qc_t, kc, vc_t, sink_tab, n_batch=n_batch, ctx_len=ctx_len)
            cs = _merge(cs, mod_l, mod_row_c, g1, ypc, ac_t, ysc, wg, wbp, wba, wbs, wo, tm=tm_c)
            cs = _ffn(cs, mod_l, mod_row_c, g2, wfi, wfo, None, tm=tm_c)
    return xs.reshape(n_batch, seq, d)
```

```python
import functools
import math

import jax
import jax.numpy as jnp
from jax import lax
from jax.experimental import pallas as pl
from jax.experimental.pallas import tpu as pltpu

F32 = jnp.float32
BF16 = jnp.bfloat16

GRID_W = 64
NORM_EPS = 1e-6
MASK_VALUE = -1e30
N_MOD = 6
POOL_WINDOWS = (2, 4, 8, 16)
HEAD_DIM = 64
GQA_GROUP = 4
ATTN_BLOCK = 128
ROPE_THETA = 10000.0
CHUNK = 128
N_SG_GROUPS = 4
N_BRANCHES = 3
LOG2_E = math.log2(math.e)

LANES = 128
SUBLANES = 8
MXU_DIM = 256
VMEM_LIMIT_BYTES = 56 * 1024 * 1024

BF16_ROWS = 2 * SUBLANES
MOD_ROWS = BF16_ROWS
HALO = BF16_ROWS
Q_SCALE = HEAD_DIM ** -0.5 * LOG2_E
QCOLS = GQA_GROUP * ATTN_BLOCK

NT_DIMS = (((1,), (1,)), ((), ()))


def _const_spec(shape):
    nd = len(shape)
    return pl.BlockSpec(shape, lambda *_: (0,) * nd, pipeline_mode=pl.Buffered(1))


def _params(n_axes=1):
    return pltpu.CompilerParams(dimension_semantics=("arbitrary",) * n_axes,
                                vmem_limit_bytes=VMEM_LIMIT_BYTES)


def _modnorm(xf, gain_scale, shift):
    ms = jnp.mean(xf * xf, axis=-1, keepdims=True)
    return (xf * lax.rsqrt(ms + NORM_EPS)) * gain_scale + shift


def _gelu_tanh(x):
    c0 = -2.0 * math.sqrt(2.0 / math.pi) * LOG2_E
    c1 = c0 * 0.044715
    return x / (1.0 + jnp.exp2(x * (c0 + c1 * (x * x))))


def _mod_kernel(c_ref, w_ref, b_ref, o_ref):
    c = c_ref[...]
    s = (c * jax.nn.sigmoid(c)).astype(BF16)
    o_ref[0] = jnp.dot(s, w_ref[0].astype(BF16), preferred_element_type=F32) + b_ref[0]


def _modulation(cc, w_mod, b_mod):
    depth, d, n = w_mod.shape
    tn = 1024
    return pl.pallas_call(
        _mod_kernel,
        grid=(depth, n // tn),
        in_specs=[
            pl.BlockSpec((MOD_ROWS, d), lambda l, j: (0, 0)),
            pl.BlockSpec((1, d, tn), lambda l, j: (l, 0, j)),
            pl.BlockSpec((1, 1, tn), lambda l, j: (l, 0, j)),
        ],
        out_specs=pl.BlockSpec((1, MOD_ROWS, tn), lambda l, j: (l, 0, j)),
        out_shape=jax.ShapeDtypeStruct((depth, MOD_ROWS, n), F32),
        compiler_params=_params(2),
        name="modulation",
    )(cc, w_mod, b_mod.reshape(depth, 1, n))


def _inproj_kernel(*refs, tm, seq, rope, d_pool, d_attn, d_kv, d_sg):
    if rope:
        (xp_ref, x_ref, xn_ref, mod_ref, g1_ref, wa_ref, wqv_ref, wpool_ref, pscale_ref, sgg_ref, wsp_ref, bsp_ref,
         cos_ref, sin_ref, cos_t_ref, sin_t_ref, ypool_ref, qt_ref, k_ref, vt_ref, ysg_ref) = refs
    else:
        (xp_ref, x_ref, xn_ref, mod_ref, g1_ref, wa_ref, wqv_ref, wpool_ref, pscale_ref, sgg_ref, wsp_ref, bsp_ref,
         ypool_ref, qt_ref, k_ref, vt_ref, ysg_ref) = refs
    n_ext = tm + 2 * HALO
    tiles_per_seq = seq // tm
    t_in_seq = lax.rem(pl.program_id(0), tiles_per_seq)
    start = t_in_seq * tm

    shift = mod_ref[0, 0:1, :]
    gain_scale = g1_ref[...] * (1.0 + mod_ref[0, 1:2, :])
    h_bf = _modnorm(x_ref[...], gain_scale, shift).astype(BF16)
    h_ext = jnp.concatenate([_modnorm(xp_ref[...], gain_scale, shift).astype(BF16), h_bf,
                             _modnorm(xn_ref[...], gain_scale, shift).astype(BF16)], axis=0)
    zp = jnp.dot(h_ext, wa_ref[:, :d_pool], preferred_element_type=F32)
    zr = jnp.dot(h_bf, wa_ref[:, d_pool:], preferred_element_type=F32)
    zt = lax.dot_general(wqv_ref[...], h_bf, NT_DIMS, preferred_element_type=F32)

    halo_row = lax.broadcasted_iota(jnp.int32, (HALO, LANES), 0)
    keep_head = halo_row >= jnp.where(t_in_seq == 0, HALO, 0)
    keep_tail = halo_row < jnp.where(t_in_seq == tiles_per_seq - 1, 0, HALO)
    edge = SUBLANES
    edge_row = lax.broadcasted_iota(jnp.int32, (edge, LANES), 0)
    pos_head = edge_row + start
    pos_tail = edge_row + (start + tm - edge)

    def shifted(a, s):
        return pltpu.roll(a, s % n_ext, 0)

    def inv_count(pos, half):
        return 1.0 / (jnp.minimum(pos + half, seq) - jnp.maximum(pos - half, 0)).astype(F32)

    for g, w in enumerate(POOL_WINDOWS):
        half = w // 2
        cols = slice(g * LANES, (g + 1) * LANES)
        e = jnp.concatenate([jnp.where(keep_head, zp[:HALO, cols], 0.0), zp[HALO:HALO + tm, cols],
                             jnp.where(keep_tail, zp[HALO + tm:, cols], 0.0)], axis=0)
        trail = e
        s = 1
        while s < half:
            trail = trail + shifted(trail, s)
            s *= 2
        wsum = shifted(trail, -(half - 1)) + shifted(trail, 1) if half > 1 else trail + shifted(trail, 1)
        ws = wsum[HALO:HALO + tm]
        own = e[HALO:HALO + tm]
        pooled = jnp.concatenate([
            ws[:edge] * inv_count(pos_head, half) - own[:edge],
            ws[edge:tm - edge] * (1.0 / w) - own[edge:tm - edge],
            ws[tm - edge:] * inv_count(pos_tail, half) - own[tm - edge:]], axis=0)
        mixed = jnp.dot(pooled.astype(BF16), wpool_ref[g], preferred_element_type=F32)
        ypool_ref[:, cols] = (mixed * pscale_ref[:, cols]).astype(BF16)

    quarter = HEAD_DIM // 4
    for hq in range(d_attn // HEAD_DIM):
        blk = zt[hq * HEAD_DIM:(hq + 1) * HEAD_DIM]
        if rope:
            swapped = jnp.concatenate([blk[quarter:2 * quarter], blk[:quarter],
                                       blk[3 * quarter:], blk[2 * quarter:3 * quarter]], axis=0)
            blk = blk * cos_t_ref[...] + swapped * sin_t_ref[...]
        else:
            blk = blk * Q_SCALE
        qt_ref[hq * HEAD_DIM:(hq + 1) * HEAD_DIM, :] = blk.astype(BF16)
    vt_ref[...] = zt[d_attn:d_attn + d_kv].astype(BF16)

    zk = zr[:, :d_kv]
    if rope:
        lane = lax.broadcasted_iota(jnp.int32, (tm, LANES), 1)
        first16 = (lane % (2 * quarter)) < quarter
        swapped = jnp.where(first16, pltpu.roll(zk, LANES - quarter, 1), pltpu.roll(zk, quarter, 1))
        zk = zk * cos_ref[...] + swapped * sin_ref[...]
    k_ref[...] = zk.astype(BF16)

    o_u = d_kv
    o_sv = o_u + d_sg
    gu = _gelu_tanh(zr[:, o_u:o_u + d_sg])
    gs = _gelu_tanh(zr[:, o_sv:o_sv + d_sg])
    ms = jnp.mean(gs * gs, axis=-1, keepdims=True)
    vn = ((gs * lax.rsqrt(ms + NORM_EPS)) * sgg_ref[...]).astype(BF16)
    for c in range(tm // CHUNK):
        r = slice(c * CHUNK, (c + 1) * CHUNK)
        for g in range(N_SG_GROUPS):
            cols = slice(g * LANES, (g + 1) * LANES)
            mixed = jnp.dot(wsp_ref[g], vn[r, cols], preferred_element_type=F32) + bsp_ref[:, cols]
            ysg_ref[r, cols] = (gu[r, cols] * mixed).astype(BF16)


def _inproj(x2, mod_l, mod_row_fn, g1, wa, wqv, wpool, pscale, sgg, wsp, bsp, rope_tabs, *, tm, seq, dims):
    t, d = x2.shape
    d_pool, d_attn, d_kv, d_sg = dims
    n_tiles = t // tm
    assert max(POOL_WINDOWS) // 2 <= SUBLANES <= HALO and tm % HALO == 0
    hb = tm // HALO
    n_hblk = t // HALO
    tiles_per_seq = seq // tm
    rope = rope_tabs is not None
    in_specs = [
        pl.BlockSpec((HALO, d), lambda i: (jnp.maximum(i * hb - 1, 0), 0)),
        pl.BlockSpec((tm, d), lambda i: (i, 0)),
        pl.BlockSpec((HALO, d), lambda i: (jnp.minimum((i + 1) * hb, n_hblk - 1), 0)),
        pl.BlockSpec((1, N_MOD, d), lambda i: (mod_row_fn(i), 0, 0)),
        _const_spec(g1.shape), _const_spec(wa.shape), _const_spec(wqv.shape), _const_spec(wpool.shape),
        _const_spec(pscale.shape), _const_spec(sgg.shape), _const_spec(wsp.shape), _const_spec(bsp.shape),
    ]
    args = [x2, x2, x2, mod_l, g1, wa, wqv, wpool, pscale, sgg, wsp, bsp]
    if rope:
        in_specs += [pl.BlockSpec((tm, LANES), lambda i: (lax.rem(i, tiles_per_seq), 0))] * 2
        in_specs += [pl.BlockSpec((HEAD_DIM, tm), lambda i: (0, lax.rem(i, tiles_per_seq)))] * 2
        args += list(rope_tabs)
    row = lambda i: (i, 0)
    col = lambda i: (0, i)
    return pl.pallas_call(
        functools.partial(_inproj_kernel, tm=tm, seq=seq, rope=rope, d_pool=d_pool, d_attn=d_attn,
                          d_kv=d_kv, d_sg=d_sg),
        grid=(n_tiles,),
        in_specs=in_specs,
        out_specs=[pl.BlockSpec((tm, d_pool), row), pl.BlockSpec((d_attn, tm), col), pl.BlockSpec((tm, d_kv), row),
                   pl.BlockSpec((d_kv, tm), col), pl.BlockSpec((tm, d_sg), row)],
        out_shape=[jax.ShapeDtypeStruct((t, d_pool), BF16), jax.ShapeDtypeStruct((d_attn, t), BF16),
                   jax.ShapeDtypeStruct((t, d_kv), BF16), jax.ShapeDtypeStruct((d_kv, t), BF16),
                   jax.ShapeDtypeStruct((t, d_sg), BF16)],
        compiler_params=_params(1),
        name="inproj_rope" if rope else "inproj_ctx",
    )(*args)


def _attend_t(q_rhs, k_cat, vt_cat, masks, sink_row):
    s = jnp.dot(k_cat, q_rhs, preferred_element_type=F32)
    if masks:
        pieces, at = [], 0
        for rs, mk in masks:
            if rs.start > at:
                pieces.append(s[at:rs.start])
            pieces.append(jnp.where(mk, s[rs], MASK_VALUE))
            at = rs.stop
        if at < s.shape[0]:
            pieces.append(s[at:])
        s = jnp.concatenate(pieces, axis=0)
    m = jnp.maximum(jnp.max(s, axis=0, keepdims=True), sink_row)
    p = jnp.exp2(s - m)
    denom = jnp.sum(p, axis=0, keepdims=True) + jnp.exp2(sink_row - m)
    o = jnp.dot(vt_cat, p.astype(BF16), preferred_element_type=F32)
    return o / denom


def _q_rhs(qt_ref, h, cols):
    blocks = [qt_ref[(GQA_GROUP * h + j) * HEAD_DIM:(GQA_GROUP * h + j + 1) * HEAD_DIM, cols]
              for j in range(GQA_GROUP)]
    q = jnp.concatenate(blocks, axis=1)
    z = jnp.zeros_like(q)
    return jnp.concatenate([q, z] if h == 0 else [z, q], axis=0)


def _store_heads(o_ref, h, rows, out):
    for j in range(0, GQA_GROUP, 2):
        pair = jnp.concatenate([out[:, j * ATTN_BLOCK:(j + 1) * ATTN_BLOCK],
                                out[:, (j + 1) * ATTN_BLOCK:(j + 2) * ATTN_BLOCK]], axis=0)
        lo = (GQA_GROUP * h + j) * HEAD_DIM
        o_ref[rows, lo:lo + 2 * HEAD_DIM] = pair.T.astype(BF16)


def _win_attn_kernel(qt_ref, kp_ref, ko_ref, kn_ref, vp_ref, vo_ref, vn_ref, kc_ref, vc_ref, sink_ref, o_ref,
                     *, qt, n_steps):
    step = pl.program_id(1)
    nsub = qt // ATTN_BLOCK
    key = lax.broadcasted_iota(jnp.int32, (ATTN_BLOCK, QCOLS), 0)
    qry = lax.rem(lax.broadcasted_iota(jnp.int32, (ATTN_BLOCK, QCOLS), 1), ATTN_BLOCK)
    tri_prev = key >= qry
    tri_next = key <= qry
    edge_prev = key >= qry + jnp.where(step > 0, 0, 2 * ATTN_BLOCK)
    edge_next = key <= qry - jnp.where(step < n_steps - 1, 0, 2 * ATTN_BLOCK)
    prev_rows = slice(0, ATTN_BLOCK)
    next_rows = slice(2 * ATTN_BLOCK, 3 * ATTN_BLOCK)
    for h in range(2):
        feat = slice(h * HEAD_DIM, (h + 1) * HEAD_DIM)
        sink_row = sink_ref[h] * LOG2_E
        for n in range(nsub):
            r = slice(n * ATTN_BLOCK, (n + 1) * ATTN_BLOCK)
            if n > 0:
                rp = slice((n - 1) * ATTN_BLOCK, n * ATTN_BLOCK)
                k_prev, v_prev, m_prev = ko_ref[rp, :], vo_ref[feat, rp], tri_prev
            else:
                k_prev, v_prev, m_prev = kp_ref[...], vp_ref[feat, :], edge_prev
            if n < nsub - 1:
                rn = slice((n + 1) * ATTN_BLOCK, (n + 2) * ATTN_BLOCK)
                k_next, v_next, m_next = ko_ref[rn, :], vo_ref[feat, rn], tri_next
            else:
                k_next, v_next, m_next = kn_ref[...], vn_ref[feat, :], edge_next
            k_cat = jnp.concatenate([k_prev, ko_ref[r, :], k_next, kc_ref[...]], axis=0)
            vt_cat = jnp.concatenate([v_prev, vo_ref[feat, r], v_next, vc_ref[feat, :]], axis=1)
            out = _attend_t(_q_rhs(qt_ref, h, r), k_cat, vt_cat,
                            [(prev_rows, m_prev), (next_rows, m_next)], sink_row)
            _store_heads(o_ref, h, r, out)


def _window_attention(q_t, k, v_t, kc, vc_t, sink_tab, *, n_batch, seq, ctx_len, qt):
    dq, t = q_t.shape
    dkv = k.shape[1]
    n_steps = seq // qt
    sub = qt // ATTN_BLOCK
    blocks_per_seq = seq // ATTN_BLOCK
    own = lambda b, j: b * n_steps + j
    prev = lambda b, j: b * blocks_per_seq + jnp.maximum(j * sub - 1, 0)
    nxt = lambda b, j: b * blocks_per_seq + jnp.minimum((j + 1) * sub, blocks_per_seq - 1)
    rows = lambda f: (lambda b, j: (f(b, j), 0))
    cols = lambda f: (lambda b, j: (0, f(b, j)))
    return pl.pallas_call(
        functools.partial(_win_attn_kernel, qt=qt, n_steps=n_steps),
        grid=(n_batch, n_steps),
        in_specs=[
            pl.BlockSpec((dq, qt), cols(own)),
            pl.BlockSpec((ATTN_BLOCK, dkv), rows(prev)), pl.BlockSpec((qt, dkv), rows(own)),
            pl.BlockSpec((ATTN_BLOCK, dkv), rows(nxt)),
            pl.BlockSpec((dkv, ATTN_BLOCK), cols(prev)), pl.BlockSpec((dkv, qt), cols(own)),
            pl.BlockSpec((dkv, ATTN_BLOCK), cols(nxt)),
            pl.BlockSpec((ctx_len, dkv), lambda b, j: (b, 0)), pl.BlockSpec((dkv, ctx_len), lambda b, j: (0, b)),
            pl.BlockSpec(sink_tab.shape, lambda b, j: (0, 0, 0)),
        ],
        out_specs=pl.BlockSpec((qt, dq), rows(own)),
        out_shape=jax.ShapeDtypeStruct((t, dq), BF16),
        compiler_params=_params(2),
        name="window_attention",
    )(q_t, k, k, k, v_t, v_t, v_t, kc, vc_t, sink_tab)


def _ctx_attn_kernel(qt_ref, k_ref, vt_ref, sink_ref, o_ref, *, ctx_len):
    for h in range(2):
        feat = slice(h * HEAD_DIM, (h + 1) * HEAD_DIM)
        sink_row = sink_ref[h] * LOG2_E
        for n in range(ctx_len // ATTN_BLOCK):
            r = slice(n * ATTN_BLOCK, (n + 1) * ATTN_BLOCK)
            out = _attend_t(_q_rhs(qt_ref, h, r), k_ref[...], vt_ref[feat, :], [], sink_row)
            _store_heads(o_ref, h, r, out)


def _context_attention(q_t, k, v_t, sink_tab, *, n_batch, ctx_len):
    dq, t = q_t.shape
    dkv = k.shape[1]
    return pl.pallas_call(
        functools.partial(_ctx_attn_kernel, ctx_len=ctx_len),
        grid=(n_batch,),
        in_specs=[pl.BlockSpec((dq, ctx_len), lambda b: (0, b)), pl.BlockSpec((ctx_len, dkv), lambda b: (b, 0)),
                  pl.BlockSpec((dkv, ctx_len), lambda b: (0, b)), pl.BlockSpec(sink_tab.shape, lambda b: (0, 0, 0))],
        out_specs=pl.BlockSpec((ctx_len, dq), lambda b: (b, 0)),
        out_shape=jax.ShapeDtypeStruct((t, dq), BF16),
        compiler_params=_params(1),
        name="context_attention",
    )(q_t, k, v_t, sink_tab)


def _row_parts(tm, sub):
    return [slice(r, r + sub) for r in range(0, tm, sub)]


def _merge_kernel(x_ref, mod_ref, g1_ref, yp_ref, ya_ref, ys_ref, wg_ref, wbp_ref, wba_ref, wbs_ref, wo_ref, o_ref,
                  *, sub):
    tm, d = x_ref.shape
    shift, gate = mod_ref[0, 0:1, :], mod_ref[0, 2:3, :]
    gain_scale = g1_ref[...] * (1.0 + mod_ref[0, 1:2, :])
    for r in _row_parts(tm, sub):
        x = x_ref[r, :]
        hx = _modnorm(x, gain_scale, shift).astype(BF16)
        y = None
        for b, (y_ref, w_ref) in enumerate(((yp_ref, wbp_ref), (ya_ref, wba_ref), (ys_ref, wbs_ref))):
            gate_b = jax.nn.sigmoid(jnp.dot(hx, wg_ref[:, b * d:(b + 1) * d], preferred_element_type=F32))
            part = gate_b * jnp.dot(y_ref[r, :], w_ref[...], preferred_element_type=F32)
            y = part if y is None else y + part
        o = jnp.dot(y.astype(BF16), wo_ref[...], preferred_element_type=F32)
        o_ref[r, :] = x + gate * o


def _merge(x2, mod_l, mod_row_fn, g1, yp, ya, ys, wg, wbp, wba, wbs, wo, *, tm, sub):
    t, d = x2.shape
    row = lambda i: (i, 0)
    return pl.pallas_call(
        functools.partial(_merge_kernel, sub=sub),
        grid=(t // tm,),
        in_specs=[
            pl.BlockSpec((tm, d), row),
            pl.BlockSpec((1, N_MOD, d), lambda i: (mod_row_fn(i), 0, 0)),
            _const_spec(g1.shape),
            pl.BlockSpec((tm, yp.shape[1]), row), pl.BlockSpec((tm, ya.shape[1]), row),
            pl.BlockSpec((tm, ys.shape[1]), row),
            _const_spec(wg.shape), _const_spec(wbp.shape), _const_spec(wba.shape), _const_spec(wbs.shape),
            _const_spec(wo.shape),
        ],
        out_specs=pl.BlockSpec((tm, d), row),
        out_shape=jax.ShapeDtypeStruct((t, d), F32),
        compiler_params=_params(1),
        name="merge",
    )(x2, mod_l, g1, yp, ya, ys, wg, wbp, wba, wbs, wo)


def _ffn_chunks(d_ff):
    n_tiles = d_ff // MXU_DIM
    first = (n_tiles + 1) // 2 * MXU_DIM
    return ((0, first), (first, d_ff))


def _ffn_kernel(*refs, d_ff, final, sub):
    if final:
        x_ref, mod_ref, g2_ref, wi_ref, wo_ref, fg_ref, o_ref = refs
    else:
        x_ref, mod_ref, g2_ref, wi_ref, wo_ref, o_ref = refs
    shift, gate = mod_ref[0, 3:4, :], mod_ref[0, 5:6, :]
    gain_scale = g2_ref[...] * (1.0 + mod_ref[0, 4:5, :])
    for r in _row_parts(x_ref.shape[0], sub):
        x = x_ref[r, :]
        h = _modnorm(x, gain_scale, shift).astype(BF16)
        acc = None
        for lo, hi in _ffn_chunks(d_ff):
            a = jnp.dot(h, wi_ref[:, lo:hi], preferred_element_type=F32)
            b = jnp.dot(h, wi_ref[:, d_ff + lo:d_ff + hi], preferred_element_type=F32)
            act = ((a * jax.nn.sigmoid(a)) * b).astype(BF16)
            part = jnp.dot(act, wo_ref[lo:hi, :], preferred_element_type=F32)
            acc = part if acc is None else acc + part
        y = x + gate * acc
        if final:
            ms = jnp.mean(y * y, axis=-1, keepdims=True)
            y = (y * lax.rsqrt(ms + NORM_EPS)) * fg_ref[...]
        o_ref[r, :] = y


def _ffn(x2, mod_l, mod_row_fn, g2, wi, wo, final_gain, *, tm, sub):
    t, d = x2.shape
    d_ff = wo.shape[0]
    final = final_gain is not None
    row = lambda i: (i, 0)
    in_specs = [
        pl.BlockSpec((tm, d), row),
        pl.BlockSpec((1, N_MOD, d), lambda i: (mod_row_fn(i), 0, 0)),
        _const_spec(g2.shape), _const_spec(wi.shape), _const_spec(wo.shape),
    ]
    args = [x2, mod_l, g2, wi, wo]
    if final:
        in_specs.append(_const_spec(final_gain.shape))
        args.append(final_gain)
    return pl.pallas_call(
        functools.partial(_ffn_kernel, d_ff=d_ff, final=final, sub=sub),
        grid=(t // tm,),
        in_specs=in_specs,
        out_specs=pl.BlockSpec((tm, d), row),
        out_shape=jax.ShapeDtypeStruct((t, d), F32),
        compiler_params=_params(1),
        name="ffn_final" if final else "ffn",
    )(*args)


def _rope_tables(seq):
    rows = seq // GRID_W
    freqs = HEAD_DIM // 4
    row = jnp.repeat(jnp.arange(rows), GRID_W).astype(F32)
    col = jnp.tile(jnp.arange(GRID_W), rows).astype(F32)
    inv_freq = ROPE_THETA ** (-jnp.arange(freqs, dtype=F32) / freqs)
    ang_r = row[:, None] * inv_freq[None, :]
    ang_c = col[:, None] * inv_freq[None, :]
    cos = jnp.concatenate([jnp.cos(ang_r), jnp.cos(ang_r), jnp.cos(ang_c), jnp.cos(ang_c)], axis=-1)
    sin = jnp.concatenate([-jnp.sin(ang_r), jnp.sin(ang_r), -jnp.sin(ang_c), jnp.sin(ang_c)], axis=-1)
    reps = LANES // HEAD_DIM
    return jnp.tile(cos, (1, reps)), jnp.tile(sin, (1, reps)), cos.T * Q_SCALE, sin.T * Q_SCALE


def _pick_tile(seq, target):
    tm = min(seq, target)
    assert seq % tm == 0 and tm % CHUNK == 0
    return tm


def kernel(x, c, ctx, c_ctx, w_mod, b_mod, norm1_gain, norm2_gain, w_in, w_pool, pool_scale, attn_sink, sg_v_gain,
           w_spatial, b_spatial, w_br_pool, w_br_attn, w_br_sg, w_out, w_ffn_in, w_ffn_out, final_gain):
    n_batch, seq, d = x.shape
    ctx_len = ctx.shape[1]
    depth = w_mod.shape[0]
    d_pool = w_br_pool.shape[1]
    d_attn = w_br_attn.shape[1]
    d_sg = w_br_sg.shape[1]
    n_q_heads = attn_sink.shape[1]
    d_kv = (n_q_heads // GQA_GROUP) * HEAD_DIM
    off_k = d_pool + d_attn
    off_v = off_k + d_kv
    off_u = off_v + d_kv
    off_gate = off_u + 2 * d_sg
    assert n_q_heads == 2 * GQA_GROUP and d_kv == LANES
    assert d_pool == len(POOL_WINDOWS) * LANES and d_sg == N_SG_GROUPS * LANES
    assert w_in.shape[2] == off_gate + N_BRANCHES * d and n_batch + 1 <= MOD_ROWS
    assert seq % GRID_W == 0 and w_spatial.shape[-1] == CHUNK
    dims = (d_pool, d_attn, d_kv, d_sg)

    tm_x = _pick_tile(seq, 512)
    tm_c = _pick_tile(ctx_len, 512)
    qt = _pick_tile(seq, 512)
    big_x = _pick_tile(seq, 1024)
    big_c = _pick_tile(n_batch * ctx_len, 1024)
    sub_x, sub_c = min(big_x, 512), min(big_c, 512)

    cc = jnp.concatenate([c, c_ctx[None, :], jnp.zeros((MOD_ROWS - n_batch - 1, d), F32)], axis=0)
    mod = _modulation(cc, w_mod, b_mod).reshape(depth, MOD_ROWS, N_MOD, d)

    rope_tabs = _rope_tables(seq)
    x_tiles_per_seq = seq // tm_x
    big_tiles_per_seq = seq // big_x
    mod_row_x = lambda i: i // x_tiles_per_seq
    mod_row_big = lambda i: i // big_tiles_per_seq
    mod_row_c = lambda i: n_batch

    xs = x.reshape(n_batch * seq, d)
    cs = ctx.reshape(n_batch * ctx_len, d)
    row2 = lambda a: a.reshape(1, -1)

    for l in range(depth):
        last = l == depth - 1
        mod_l = mod[l]
        g1, g2 = row2(norm1_gain[l]), row2(norm2_gain[l])
        w_l = w_in[l]
        wa = jnp.concatenate([w_l[:, :d_pool], w_l[:, off_k:off_v], w_l[:, off_u:off_gate]], axis=1).astype(BF16)
        wqv = jnp.concatenate([w_l[:, d_pool:off_k], w_l[:, off_v:off_u]], axis=1).T.astype(BF16)
        wg = w_l[:, off_gate:].astype(BF16)
        wpool = w_pool[l].astype(BF16)
        wsp = w_spatial[l].astype(BF16)
        pscale, sgg = row2(pool_scale[l]), row2(sg_v_gain[l])
        bsp = jnp.repeat(b_spatial[l].T, LANES, axis=1)
        sink_tab = jnp.repeat(attn_sink[l].reshape(2, GQA_GROUP), ATTN_BLOCK, axis=1)[:, None, :]
        wbp, wba, wbs = (w_br_pool[l].astype(BF16), w_br_attn[l].astype(BF16), w_br_sg[l].astype(BF16))
        wo = w_out[l].astype(BF16)
        wfi, wfo = w_ffn_in[l].astype(BF16), w_ffn_out[l].astype(BF16)

        common = (g1, wa, wqv, wpool, pscale, sgg, wsp, bsp)
        ypc, qc_t, kc, vc_t, ysc = _inproj(cs, mod_l, mod_row_c, *common, None, tm=tm_c, seq=ctx_len, dims=dims)
        ypx, qx_t, kx, vx_t, ysx = _inproj(xs, mod_l, mod_row_x, *common, rope_tabs, tm=tm_x, seq=seq, dims=dims)
        ax = _window_attention(qx_t, kx, vx_t, kc, vc_t, sink_tab, n_batch=n_batch, seq=seq, ctx_len=ctx_len, qt=qt)
        xs = _merge(xs, mod_l, mod_row_big, g1, ypx, ax, ysx, wg, wbp, wba, wbs, wo, tm=big_x, sub=sub_x)
        xs = _ffn(xs, mod_l, mod_row_big, g2, wfi, wfo, row2(final_gain) if last else None, tm=big_x, sub=sub_x)
        if not last:
            ac = _context_attention(qc_t, kc, vc_t, sink_tab, n_batch=n_batch, ctx_len=ctx_len)
            cs = _merge(cs, mod_l, mod_row_c, g1, ypc, ac, ysc, wg, wbp, wba, wbs, wo, tm=big_c, sub=sub_c)
            cs = _ffn(cs, mod_l, mod_row_c, g2, wfi, wfo, None, tm=big_c, sub=sub_c)
    return xs.reshape(n_batch, seq, d)
```

```python
import functools
import math

import jax
import jax.numpy as jnp
from jax import lax
from jax.experimental import pallas as pl
from jax.experimental.pallas import tpu as pltpu

F32 = jnp.float32
BF16 = jnp.bfloat16

GRID_W = 64
NORM_EPS = 1e-6
MASK_VALUE = -1e30
N_MOD = 6
POOL_WINDOWS = (2, 4, 8, 16)
HEAD_DIM = 64
GQA_GROUP = 4
ATTN_BLOCK = 128
ROPE_THETA = 10000.0
CHUNK = 128
N_SG_GROUPS = 4
N_BRANCHES = 3
LOG2_E = math.log2(math.e)

LANES = 128
SUBLANES = 8
MXU_DIM = 256
VMEM_LIMIT_BYTES = 56 * 1024 * 1024

BF16_ROWS = 2 * SUBLANES
MOD_ROWS = BF16_ROWS
HALO = BF16_ROWS
Q_SCALE = HEAD_DIM ** -0.5 * LOG2_E
QCOLS = GQA_GROUP * ATTN_BLOCK

NT_DIMS = (((1,), (1,)), ((), ()))


def _const_spec(shape):
    nd = len(shape)
    return pl.BlockSpec(shape, lambda *_: (0,) * nd, pipeline_mode=pl.Buffered(1))


def _params(n_axes=1):
    return pltpu.CompilerParams(dimension_semantics=("arbitrary",) * n_axes,
                                vmem_limit_bytes=VMEM_LIMIT_BYTES)


def _modnorm(xf, gain_scale, shift):
    ms = jnp.mean(xf * xf, axis=-1, keepdims=True)
    return (xf * lax.rsqrt(ms + NORM_EPS)) * gain_scale + shift


def _gelu_tanh(x):
    c0 = -2.0 * math.sqrt(2.0 / math.pi) * LOG2_E
    c1 = c0 * 0.044715
    return x / (1.0 + jnp.exp2(x * (c0 + c1 * (x * x))))


def _mod_kernel(c_ref, w_ref, b_ref, o_ref):
    c = c_ref[...]
    s = (c * jax.nn.sigmoid(c)).astype(BF16)
    o_ref[0] = jnp.dot(s, w_ref[0].astype(BF16), preferred_element_type=F32) + b_ref[0]


def _modulation(cc, w_mod, b_mod):
    depth, d, n = w_mod.shape
    tn = 1024
    return pl.pallas_call(
        _mod_kernel,
        grid=(depth, n // tn),
        in_specs=[
            pl.BlockSpec((MOD_ROWS, d), lambda l, j: (0, 0)),
            pl.BlockSpec((1, d, tn), lambda l, j: (l, 0, j)),
            pl.BlockSpec((1, 1, tn), lambda l, j: (l, 0, j)),
        ],
        out_specs=pl.BlockSpec((1, MOD_ROWS, tn), lambda l, j: (l, 0, j)),
        out_shape=jax.ShapeDtypeStruct((depth, MOD_ROWS, n), F32),
        compiler_params=_params(2),
        name="modulation",
    )(cc, w_mod, b_mod.reshape(depth, 1, n))


def _inproj_kernel(*refs, tm, seq, rope, d_pool, d_attn, d_kv, d_sg):
    if rope:
        (xp_ref, x_ref, xn_ref, mod_ref, g1_ref, wa_ref, wqvk_ref, wpool_ref, pscale_ref, sgg_ref, wsp_ref, bsp_ref,
         cos_q_ref, sin_q_ref, cos_k_ref, sin_k_ref, ypool_ref, qt_ref, k_ref, vt_ref, ysg_ref) = refs
    else:
        (xp_ref, x_ref, xn_ref, mod_ref, g1_ref, wa_ref, wqvk_ref, wpool_ref, pscale_ref, sgg_ref, wsp_ref, bsp_ref,
         ypool_ref, qt_ref, k_ref, vt_ref, ysg_ref) = refs
    n_ext = tm + 2 * HALO
    tiles_per_seq = seq // tm
    t_in_seq = lax.rem(pl.program_id(0), tiles_per_seq)
    start = t_in_seq * tm

    shift = mod_ref[0, 0:1, :]
    gain_scale = g1_ref[...] * (1.0 + mod_ref[0, 1:2, :])
    h_bf = _modnorm(x_ref[...], gain_scale, shift).astype(BF16)
    h_ext = jnp.concatenate([_modnorm(xp_ref[...], gain_scale, shift).astype(BF16), h_bf,
                             _modnorm(xn_ref[...], gain_scale, shift).astype(BF16)], axis=0)
    zp = jnp.dot(h_ext, wa_ref[:, :d_pool], preferred_element_type=F32)
    zr = jnp.dot(h_bf, wa_ref[:, d_pool:], preferred_element_type=F32)
    zt = lax.dot_general(wqvk_ref[...], h_bf, NT_DIMS, preferred_element_type=F32)

    halo_row = lax.broadcasted_iota(jnp.int32, (HALO, LANES), 0)
    keep_head = halo_row >= jnp.where(t_in_seq == 0, HALO, 0)
    keep_tail = halo_row < jnp.where(t_in_seq == tiles_per_seq - 1, 0, HALO)
    edge = SUBLANES
    edge_row = lax.broadcasted_iota(jnp.int32, (edge, LANES), 0)
    pos_head = edge_row + start
    pos_tail = edge_row + (start + tm - edge)

    def shifted(a, s):
        return pltpu.roll(a, s % n_ext, 0)

    def inv_count(pos, half):
        return 1.0 / (jnp.minimum(pos + half, seq) - jnp.maximum(pos - half, 0)).astype(F32)

    pooled = []
    for g, w in enumerate(POOL_WINDOWS):
        half = w // 2
        cols = slice(g * LANES, (g + 1) * LANES)
        e = jnp.concatenate([jnp.where(keep_head, zp[:HALO, cols], 0.0), zp[HALO:HALO + tm, cols],
                             jnp.where(keep_tail, zp[HALO + tm:, cols], 0.0)], axis=0)
        trail = e
        s = 1
        while s < half:
            trail = trail + shifted(trail, s)
            s *= 2
        wsum = shifted(trail, -(half - 1)) + shifted(trail, 1) if half > 1 else trail + shifted(trail, 1)
        ws = wsum[HALO:HALO + tm]
        own = e[HALO:HALO + tm]
        pooled.append(jnp.concatenate([
            ws[:edge] * inv_count(pos_head, half) - own[:edge],
            ws[edge:tm - edge] * (1.0 / w) - own[edge:tm - edge],
            ws[tm - edge:] * inv_count(pos_tail, half) - own[tm - edge:]], axis=0).astype(BF16))
    for pair in range(len(POOL_WINDOWS) // 2):
        cols = slice(2 * pair * LANES, 2 * (pair + 1) * LANES)
        mixed = jnp.dot(jnp.concatenate(pooled[2 * pair:2 * pair + 2], axis=1), wpool_ref[pair],
                        preferred_element_type=F32)
        ypool_ref[:, cols] = (mixed * pscale_ref[:, cols]).astype(BF16)

    quarter = HEAD_DIM // 4

    def head_t(i):
        return zt[i * HEAD_DIM:(i + 1) * HEAD_DIM]

    def rotate_t(blk, cos_ref, sin_ref):
        swapped = jnp.concatenate([blk[quarter:2 * quarter], blk[:quarter],
                                   blk[3 * quarter:], blk[2 * quarter:3 * quarter]], axis=0)
        return blk * cos_ref[...] + swapped * sin_ref[...]

    n_q, n_kv = d_attn // HEAD_DIM, d_kv // HEAD_DIM
    for hq in range(n_q):
        blk = head_t(hq)
        qt_ref[hq * HEAD_DIM:(hq + 1) * HEAD_DIM, :] = (rotate_t(blk, cos_q_ref, sin_q_ref) if rope else blk).astype(BF16)
    vt_ref[...] = zt[d_attn:d_attn + d_kv].astype(BF16)
    k_heads = [head_t(n_q + n_kv + hk) for hk in range(n_kv)]
    k_heads = [rotate_t(blk, cos_k_ref, sin_k_ref) if rope else blk * Q_SCALE for blk in k_heads]
    k_ref[...] = jnp.concatenate(k_heads, axis=0).T.astype(BF16)

    gu = _gelu_tanh(zr[:, :d_sg])
    gs = _gelu_tanh(zr[:, d_sg:])
    ms = jnp.mean(gs * gs, axis=-1, keepdims=True)
    vn = ((gs * lax.rsqrt(ms + NORM_EPS)) * sgg_ref[...]).astype(BF16)
    chunks = [slice(c * CHUNK, (c + 1) * CHUNK) for c in range(tm // CHUNK)]
    for g in range(N_SG_GROUPS):
        cols = slice(g * LANES, (g + 1) * LANES)
        mixed = jnp.dot(wsp_ref[g], jnp.concatenate([vn[r, cols] for r in chunks], axis=1),
                        preferred_element_type=F32)
        for r in chunks:
            ysg_ref[r, cols] = (gu[r, cols] * (mixed[:, r] + bsp_ref[:, cols])).astype(BF16)


def _inproj(x2, mod_l, mod_row_fn, g1, wa, wqv, wpool, pscale, sgg, wsp, bsp, rope_tabs, *, tm, seq, dims):
    t, d = x2.shape
    d_pool, d_attn, d_kv, d_sg = dims
    n_tiles = t // tm
    assert max(POOL_WINDOWS) // 2 <= SUBLANES <= HALO and tm % HALO == 0
    hb = tm // HALO
    n_hblk = t // HALO
    tiles_per_seq = seq // tm
    rope = rope_tabs is not None
    in_specs = [
        pl.BlockSpec((HALO, d), lambda i: (jnp.maximum(i * hb - 1, 0), 0)),
        pl.BlockSpec((tm, d), lambda i: (i, 0)),
        pl.BlockSpec((HALO, d), lambda i: (jnp.minimum((i + 1) * hb, n_hblk - 1), 0)),
        pl.BlockSpec((1, N_MOD, d), lambda i: (mod_row_fn(i), 0, 0)),
        _const_spec(g1.shape), _const_spec(wa.shape), _const_spec(wqv.shape), _const_spec(wpool.shape),
        _const_spec(pscale.shape), _const_spec(sgg.shape), _const_spec(wsp.shape), _const_spec(bsp.shape),
    ]
    args = [x2, x2, x2, mod_l, g1, wa, wqv, wpool, pscale, sgg, wsp, bsp]
    if rope:
        in_specs += [pl.BlockSpec((HEAD_DIM, tm), lambda i: (0, lax.rem(i, tiles_per_seq)))] * len(rope_tabs)
        args += list(rope_tabs)
    row = lambda i: (i, 0)
    col = lambda i: (0, i)
    return pl.pallas_call(
        functools.partial(_inproj_kernel, tm=tm, seq=seq, rope=rope, d_pool=d_pool, d_attn=d_attn,
                          d_kv=d_kv, d_sg=d_sg),
        grid=(n_tiles,),
        in_specs=in_specs,
        out_specs=[pl.BlockSpec((tm, d_pool), row), pl.BlockSpec((d_attn, tm), col), pl.BlockSpec((tm, d_kv), row),
                   pl.BlockSpec((d_kv, tm), col), pl.BlockSpec((tm, d_sg), row)],
        out_shape=[jax.ShapeDtypeStruct((t, d_pool), BF16), jax.ShapeDtypeStruct((d_attn, t), BF16),
                   jax.ShapeDtypeStruct((t, d_kv), BF16), jax.ShapeDtypeStruct((d_kv, t), BF16),
                   jax.ShapeDtypeStruct((t, d_sg), BF16)],
        compiler_params=_params(1),
        name="inproj_rope" if rope else "inproj_ctx",
    )(*args)


def _logits(q_rhs, k_cat, masks, sink_row):
    s = jnp.dot(k_cat, q_rhs, preferred_element_type=F32)
    if masks:
        pieces, at = [], 0
        for rs, mk in masks:
            if rs.start > at:
                pieces.append(s[at:rs.start])
            pieces.append(jnp.where(mk, s[rs], MASK_VALUE))
            at = rs.stop
        if at < s.shape[0]:
            pieces.append(s[at:])
        s = jnp.concatenate(pieces, axis=0)
    return s, jnp.maximum(jnp.max(s, axis=0, keepdims=True), sink_row)


def _probabilities(s, m, sink_row):
    p = jnp.exp2(s - m)
    denom = jnp.sum(p, axis=0, keepdims=True) + jnp.exp2(sink_row - m)
    return p.astype(BF16), denom


def _weighted_values(p, denom, vt_cat):
    return jnp.dot(vt_cat, p, preferred_element_type=F32) / denom


def _q_rhs(qt_ref, h, cols):
    blocks = [qt_ref[(GQA_GROUP * h + j) * HEAD_DIM:(GQA_GROUP * h + j + 1) * HEAD_DIM, cols]
              for j in range(GQA_GROUP)]
    q = jnp.concatenate(blocks, axis=1)
    z = jnp.zeros_like(q)
    return jnp.concatenate([q, z] if h == 0 else [z, q], axis=0)


def _store_heads(o_ref, h, rows, out):
    for j in range(0, GQA_GROUP, 2):
        pair = jnp.concatenate([out[:, j * ATTN_BLOCK:(j + 1) * ATTN_BLOCK],
                                out[:, (j + 1) * ATTN_BLOCK:(j + 2) * ATTN_BLOCK]], axis=0)
        lo = (GQA_GROUP * h + j) * HEAD_DIM
        o_ref[rows, lo:lo + 2 * HEAD_DIM] = pair.T.astype(BF16)


def _win_attn_kernel(qt_ref, kp_ref, ko_ref, kn_ref, vp_ref, vo_ref, vn_ref, kc_ref, vc_ref, sink_ref, o_ref,
                     *, qt, n_steps):
    step = pl.program_id(1)
    nsub = qt // ATTN_BLOCK
    key = lax.broadcasted_iota(jnp.int32, (ATTN_BLOCK, QCOLS), 0)
    qry = lax.rem(lax.broadcasted_iota(jnp.int32, (ATTN_BLOCK, QCOLS), 1), ATTN_BLOCK)
    tri_prev = key >= qry
    tri_next = key <= qry
    edge_prev = key >= qry + jnp.where(step > 0, 0, 2 * ATTN_BLOCK)
    edge_next = key <= qry - jnp.where(step < n_steps - 1, 0, 2 * ATTN_BLOCK)
    prev_rows = slice(0, ATTN_BLOCK)
    next_rows = slice(2 * ATTN_BLOCK, 3 * ATTN_BLOCK)
    sink_rows = [sink_ref[h] * LOG2_E for h in range(2)]

    def logits_stage(h, n):
        r = slice(n * ATTN_BLOCK, (n + 1) * ATTN_BLOCK)
        if n > 0:
            k_prev, m_prev = ko_ref[(n - 1) * ATTN_BLOCK:n * ATTN_BLOCK, :], tri_prev
        else:
            k_prev, m_prev = kp_ref[...], edge_prev
        if n < nsub - 1:
            k_next, m_next = ko_ref[(n + 1) * ATTN_BLOCK:(n + 2) * ATTN_BLOCK, :], tri_next
        else:
            k_next, m_next = kn_ref[...], edge_next
        k_cat = jnp.concatenate([k_prev, ko_ref[r, :], k_next, kc_ref[...]], axis=0)
        return _logits(_q_rhs(qt_ref, h, r), k_cat, [(prev_rows, m_prev), (next_rows, m_next)], sink_rows[h])

    def output_stage(h, n, p, denom):
        feat = slice(h * HEAD_DIM, (h + 1) * HEAD_DIM)
        r = slice(n * ATTN_BLOCK, (n + 1) * ATTN_BLOCK)
        v_prev = vo_ref[feat, (n - 1) * ATTN_BLOCK:n * ATTN_BLOCK] if n > 0 else vp_ref[feat, :]
        v_next = vo_ref[feat, (n + 1) * ATTN_BLOCK:(n + 2) * ATTN_BLOCK] if n < nsub - 1 else vn_ref[feat, :]
        vt_cat = jnp.concatenate([v_prev, vo_ref[feat, r], v_next, vc_ref[feat, :]], axis=1)
        _store_heads(o_ref, h, r, _weighted_values(p, denom, vt_cat))

    work = [(h, n) for h in range(2) for n in range(nsub)]
    logits, probs = {}, {}
    for i in range(len(work) + 2):
        if i < len(work):
            logits[i] = logits_stage(*work[i])
        if 0 <= i - 1 < len(work):
            probs[i - 1] = _probabilities(*logits.pop(i - 1), sink_rows[work[i - 1][0]])
        if 0 <= i - 2 < len(work):
            output_stage(*work[i - 2], *probs.pop(i - 2))


def _window_attention(q_t, k, v_t, kc, vc_t, sink_tab, *, n_batch, seq, ctx_len, qt):
    dq, t = q_t.shape
    dkv = k.shape[1]
    n_steps = seq // qt
    sub = qt // ATTN_BLOCK
    blocks_per_seq = seq // ATTN_BLOCK
    own = lambda b, j: b * n_steps + j
    prev = lambda b, j: b * blocks_per_seq + jnp.maximum(j * sub - 1, 0)
    nxt = lambda b, j: b * blocks_per_seq + jnp.minimum((j + 1) * sub, blocks_per_seq - 1)
    rows = lambda f: (lambda b, j: (f(b, j), 0))
    cols = lambda f: (lambda b, j: (0, f(b, j)))
    return pl.pallas_call(
        functools.partial(_win_attn_kernel, qt=qt, n_steps=n_steps),
        grid=(n_batch, n_steps),
        in_specs=[
            pl.BlockSpec((dq, qt), cols(own)),
            pl.BlockSpec((ATTN_BLOCK, dkv), rows(prev)), pl.BlockSpec((qt, dkv), rows(own)),
            pl.BlockSpec((ATTN_BLOCK, dkv), rows(nxt)),
            pl.BlockSpec((dkv, ATTN_BLOCK), cols(prev)), pl.BlockSpec((dkv, qt), cols(own)),
            pl.BlockSpec((dkv, ATTN_BLOCK), cols(nxt)),
            pl.BlockSpec((ctx_len, dkv), lambda b, j: (b, 0)), pl.BlockSpec((dkv, ctx_len), lambda b, j: (0, b)),
            pl.BlockSpec(sink_tab.shape, lambda b, j: (0, 0, 0)),
        ],
        out_specs=pl.BlockSpec((qt, dq), rows(own)),
        out_shape=jax.ShapeDtypeStruct((t, dq), BF16),
        compiler_params=_params(2),
        name="window_attention",
    )(q_t, k, k, k, v_t, v_t, v_t, kc, vc_t, sink_tab)


def _ctx_attn_kernel(qt_ref, k_ref, vt_ref, sink_ref, o_ref, *, ctx_len):
    sink_rows = [sink_ref[h] * LOG2_E for h in range(2)]
    blocks = [slice(n * ATTN_BLOCK, (n + 1) * ATTN_BLOCK) for n in range(ctx_len // ATTN_BLOCK)]
    work = [(h, r) for h in range(2) for r in blocks]
    logits, probs = {}, {}
    for i in range(len(work) + 2):
        if i < len(work):
            h, r = work[i]
            logits[i] = _logits(_q_rhs(qt_ref, h, r), k_ref[...], [], sink_rows[h])
        if 0 <= i - 1 < len(work):
            probs[i - 1] = _probabilities(*logits.pop(i - 1), sink_rows[work[i - 1][0]])
        if 0 <= i - 2 < len(work):
            h, r = work[i - 2]
            out = _weighted_values(*probs.pop(i - 2), vt_ref[h * HEAD_DIM:(h + 1) * HEAD_DIM, :])
            _store_heads(o_ref, h, r, out)


def _context_attention(q_t, k, v_t, sink_tab, *, n_batch, ctx_len):
    dq, t = q_t.shape
    dkv = k.shape[1]
    return pl.pallas_call(
        functools.partial(_ctx_attn_kernel, ctx_len=ctx_len),
        grid=(n_batch,),
        in_specs=[pl.BlockSpec((dq, ctx_len), lambda b: (0, b)), pl.BlockSpec((ctx_len, dkv), lambda b: (b, 0)),
                  pl.BlockSpec((dkv, ctx_len), lambda b: (0, b)), pl.BlockSpec(sink_tab.shape, lambda b: (0, 0, 0))],
        out_specs=pl.BlockSpec((ctx_len, dq), lambda b: (b, 0)),
        out_shape=jax.ShapeDtypeStruct((t, dq), BF16),
        compiler_params=_params(1),
        name="context_attention",
    )(q_t, k, v_t, sink_tab)


def _row_parts(tm, sub):
    return [slice(r, r + sub) for r in range(0, tm, sub)]


def _merge_kernel(x_ref, mod_ref, g1_ref, yp_ref, ya_ref, ys_ref, wg_ref, wbp_ref, wba_ref, wbs_ref, wo_ref, o_ref,
                  *, sub):
    tm, d = x_ref.shape
    shift, gate = mod_ref[0, 0:1, :], mod_ref[0, 2:3, :]
    gain_scale = g1_ref[...] * (1.0 + mod_ref[0, 1:2, :])
    for r in _row_parts(tm, sub):
        x = x_ref[r, :]
        hx = _modnorm(x, gain_scale, shift).astype(BF16)
        y = None
        for b, (y_ref, w_ref) in enumerate(((yp_ref, wbp_ref), (ya_ref, wba_ref), (ys_ref, wbs_ref))):
            gate_b = jax.nn.sigmoid(jnp.dot(hx, wg_ref[:, b * d:(b + 1) * d], preferred_element_type=F32))
            part = gate_b * jnp.dot(y_ref[r, :], w_ref[...], preferred_element_type=F32)
            y = part if y is None else y + part
        o = jnp.dot(y.astype(BF16), wo_ref[...], preferred_element_type=F32)
        o_ref[r, :] = x + gate * o


def _merge(x2, mod_l, mod_row_fn, g1, yp, ya, ys, wg, wbp, wba, wbs, wo, *, tm, sub):
    t, d = x2.shape
    row = lambda i: (i, 0)
    return pl.pallas_call(
        functools.partial(_merge_kernel, sub=sub),
        grid=(t // tm,),
        in_specs=[
            pl.BlockSpec((tm, d), row),
            pl.BlockSpec((1, N_MOD, d), lambda i: (mod_row_fn(i), 0, 0)),
            _const_spec(g1.shape),
            pl.BlockSpec((tm, yp.shape[1]), row), pl.BlockSpec((tm, ya.shape[1]), row),
            pl.BlockSpec((tm, ys.shape[1]), row),
            _const_spec(wg.shape), _const_spec(wbp.shape), _const_spec(wba.shape), _const_spec(wbs.shape),
            _const_spec(wo.shape),
        ],
        out_specs=pl.BlockSpec((tm, d), row),
        out_shape=jax.ShapeDtypeStruct((t, d), F32),
        compiler_params=_params(1),
        name="merge",
    )(x2, mod_l, g1, yp, ya, ys, wg, wbp, wba, wbs, wo)


def _ffn_chunks(d_ff):
    n_tiles = d_ff // MXU_DIM
    first = (n_tiles + 1) // 2 * MXU_DIM
    return ((0, first), (first, d_ff))


def _ffn_kernel(*refs, d_ff, final, sub):
    if final:
        x_ref, mod_ref, g2_ref, wi_ref, wo_ref, fg_ref, o_ref = refs
    else:
        x_ref, mod_ref, g2_ref, wi_ref, wo_ref, o_ref = refs
    shift, gate = mod_ref[0, 3:4, :], mod_ref[0, 5:6, :]
    gain_scale = g2_ref[...] * (1.0 + mod_ref[0, 4:5, :])
    for r in _row_parts(x_ref.shape[0], sub):
        x = x_ref[r, :]
        h = _modnorm(x, gain_scale, shift).astype(BF16)
        acc = None
        for lo, hi in _ffn_chunks(d_ff):
            a = jnp.dot(h, wi_ref[:, lo:hi], preferred_element_type=F32)
            b = jnp.dot(h, wi_ref[:, d_ff + lo:d_ff + hi], preferred_element_type=F32)
            act = ((a * jax.nn.sigmoid(a)) * b).astype(BF16)
            part = jnp.dot(act, wo_ref[lo:hi, :], preferred_element_type=F32)
            acc = part if acc is None else acc + part
        y = x + gate * acc
        if final:
            ms = jnp.mean(y * y, axis=-1, keepdims=True)
            y = (y * lax.rsqrt(ms + NORM_EPS)) * fg_ref[...]
        o_ref[r, :] = y


def _ffn(x2, mod_l, mod_row_fn, g2, wi, wo, final_gain, *, tm, sub):
    t, d = x2.shape
    d_ff = wo.shape[0]
    final = final_gain is not None
    row = lambda i: (i, 0)
    in_specs = [
        pl.BlockSpec((tm, d), row),
        pl.BlockSpec((1, N_MOD, d), lambda i: (mod_row_fn(i), 0, 0)),
        _const_spec(g2.shape), _const_spec(wi.shape), _const_spec(wo.shape),
    ]
    args = [x2, mod_l, g2, wi, wo]
    if final:
        in_specs.append(_const_spec(final_gain.shape))
        args.append(final_gain)
    return pl.pallas_call(
        functools.partial(_ffn_kernel, d_ff=d_ff, final=final, sub=sub),
        grid=(t // tm,),
        in_specs=in_specs,
        out_specs=pl.BlockSpec((tm, d), row),
        out_shape=jax.ShapeDtypeStruct((t, d), F32),
        compiler_params=_params(1),
        name="ffn_final" if final else "ffn",
    )(*args)


def _rope_tables(seq):
    rows = seq // GRID_W
    freqs = HEAD_DIM // 4
    row = jnp.repeat(jnp.arange(rows), GRID_W).astype(F32)
    col = jnp.tile(jnp.arange(GRID_W), rows).astype(F32)
    inv_freq = ROPE_THETA ** (-jnp.arange(freqs, dtype=F32) / freqs)
    ang_r = row[:, None] * inv_freq[None, :]
    ang_c = col[:, None] * inv_freq[None, :]
    cos = jnp.concatenate([jnp.cos(ang_r), jnp.cos(ang_r), jnp.cos(ang_c), jnp.cos(ang_c)], axis=-1)
    sin = jnp.concatenate([-jnp.sin(ang_r), jnp.sin(ang_r), -jnp.sin(ang_c), jnp.sin(ang_c)], axis=-1)
    return cos.T, sin.T, cos.T * Q_SCALE, sin.T * Q_SCALE


def _pick_tile(seq, target):
    tm = min(seq, target)
    assert seq % tm == 0 and tm % CHUNK == 0
    return tm


def kernel(x, c, ctx, c_ctx, w_mod, b_mod, norm1_gain, norm2_gain, w_in, w_pool, pool_scale, attn_sink, sg_v_gain,
           w_spatial, b_spatial, w_br_pool, w_br_attn, w_br_sg, w_out, w_ffn_in, w_ffn_out, final_gain):
    n_batch, seq, d = x.shape
    ctx_len = ctx.shape[1]
    depth = w_mod.shape[0]
    d_pool = w_br_pool.shape[1]
    d_attn = w_br_attn.shape[1]
    d_sg = w_br_sg.shape[1]
    n_q_heads = attn_sink.shape[1]
    d_kv = (n_q_heads // GQA_GROUP) * HEAD_DIM
    off_k = d_pool + d_attn
    off_v = off_k + d_kv
    off_u = off_v + d_kv
    off_gate = off_u + 2 * d_sg
    assert n_q_heads == 2 * GQA_GROUP and d_kv == LANES
    assert d_pool == len(POOL_WINDOWS) * LANES and d_sg == N_SG_GROUPS * LANES
    assert w_in.shape[2] == off_gate + N_BRANCHES * d and n_batch + 1 <= MOD_ROWS
    assert seq % GRID_W == 0 and w_spatial.shape[-1] == CHUNK
    dims = (d_pool, d_attn, d_kv, d_sg)

    tm_x = _pick_tile(seq, 512)
    tm_c = _pick_tile(ctx_len, 512)
    qt = _pick_tile(seq, 512)
    big_x = _pick_tile(seq, 1024)
    big_c = _pick_tile(n_batch * ctx_len, 1024)
    sub_x, sub_c = min(big_x, 512), min(big_c, 512)

    cc = jnp.concatenate([c, c_ctx[None, :], jnp.zeros((MOD_ROWS - n_batch - 1, d), F32)], axis=0)
    mod = _modulation(cc, w_mod, b_mod).reshape(depth, MOD_ROWS, N_MOD, d)

    rope_tabs = _rope_tables(seq)
    x_tiles_per_seq = seq // tm_x
    big_tiles_per_seq = seq // big_x
    mod_row_x = lambda i: i // x_tiles_per_seq
    mod_row_big = lambda i: i // big_tiles_per_seq
    mod_row_c = lambda i: n_batch

    xs = x.reshape(n_batch * seq, d)
    cs = ctx.reshape(n_batch * ctx_len, d)
    row2 = lambda a: a.reshape(1, -1)

    for l in range(depth):
        last = l == depth - 1
        mod_l = mod[l]
        g1, g2 = row2(norm1_gain[l]), row2(norm2_gain[l])
        w_l = w_in[l]
        wa = jnp.concatenate([w_l[:, :d_pool], w_l[:, off_u:off_gate]], axis=1).astype(BF16)
        wqv = jnp.concatenate([w_l[:, d_pool:off_k], w_l[:, off_v:off_u], w_l[:, off_k:off_v]], axis=1).T.astype(BF16)
        wg = w_l[:, off_gate:].astype(BF16)
        zblk = jnp.zeros((LANES, LANES), F32)
        wpool = jnp.stack([jnp.block([[w_pool[l, 2 * p], zblk], [zblk, w_pool[l, 2 * p + 1]]])
                           for p in range(len(POOL_WINDOWS) // 2)]).astype(BF16)
        wsp = w_spatial[l].astype(BF16)
        pscale, sgg = row2(pool_scale[l]), row2(sg_v_gain[l])
        bsp = jnp.repeat(b_spatial[l].T, LANES, axis=1)
        sink_tab = jnp.repeat(attn_sink[l].reshape(2, GQA_GROUP), ATTN_BLOCK, axis=1)[:, None, :]
        wbp, wba, wbs = (w_br_pool[l].astype(BF16), w_br_attn[l].astype(BF16), w_br_sg[l].astype(BF16))
        wo = w_out[l].astype(BF16)
        wfi, wfo = w_ffn_in[l].astype(BF16), w_ffn_out[l].astype(BF16)

        common = (g1, wa, wqv, wpool, pscale, sgg, wsp, bsp)
        ypc, qc_t, kc, vc_t, ysc = _inproj(cs, mod_l, mod_row_c, *common, None, tm=tm_c, seq=ctx_len, dims=dims)
        ypx, qx_t, kx, vx_t, ysx = _inproj(xs, mod_l, mod_row_x, *common, rope_tabs, tm=tm_x, seq=seq, dims=dims)
        ax = _window_attention(qx_t, kx, vx_t, kc, vc_t, sink_tab, n_batch=n_batch, seq=seq, ctx_len=ctx_len, qt=qt)
        xs = _merge(xs, mod_l, mod_row_big, g1, ypx, ax, ysx, wg, wbp, wba, wbs, wo, tm=big_x, sub=sub_x)
        xs = _ffn(xs, mod_l, mod_row_big, g2, wfi, wfo, row2(final_gain) if last else None, tm=big_x, sub=sub_x)
        if not last:
            ac = _context_attention(qc_t, kc, vc_t, sink_tab, n_batch=n_batch, ctx_len=ctx_len)
            cs = _merge(cs, mod_l, mod_row_c, g1, ypc, ac, ysc, wg, wbp, wba, wbs, wo, tm=big_c, sub=sub_c)
            cs = _ffn(cs, mod_l, mod_row_c, g2, wfi, wfo, None, tm=big_c, sub=sub_c)
    return xs.reshape(n_batch, seq, d)
```

```python
import functools
import math

import jax
import jax.numpy as jnp
from jax import lax
from jax.experimental import pallas as pl
from jax.experimental.pallas import tpu as pltpu

F32 = jnp.float32
BF16 = jnp.bfloat16

GRID_W = 64
NORM_EPS = 1e-6
MASK_VALUE = -1e30
N_MOD = 6
POOL_WINDOWS = (2, 4, 8, 16)
HEAD_DIM = 64
GQA_GROUP = 4
ATTN_BLOCK = 128
ROPE_THETA = 10000.0
CHUNK = 128
N_SG_GROUPS = 4
N_BRANCHES = 3
LOG2_E = math.log2(math.e)

LANES = 128
SUBLANES = 8
MXU_DIM = 256
VMEM_LIMIT_BYTES = 56 * 1024 * 1024

BF16_ROWS = 2 * SUBLANES
MOD_ROWS = BF16_ROWS
HALO = BF16_ROWS
Q_SCALE = HEAD_DIM ** -0.5 * LOG2_E
QCOLS = GQA_GROUP * ATTN_BLOCK

NT_DIMS = (((1,), (1,)), ((), ()))


def _const_spec(shape):
    nd = len(shape)
    return pl.BlockSpec(shape, lambda *_: (0,) * nd, pipeline_mode=pl.Buffered(1))


def _params(n_axes=1):
    return pltpu.CompilerParams(dimension_semantics=("arbitrary",) * n_axes,
                                vmem_limit_bytes=VMEM_LIMIT_BYTES)


def _cast_plumbing(casts, n_steps, step_of):
    in_specs, out_specs, out_shapes, splits = [], [], [], []
    for w, layer, cols in casts:
        _, r, c = w.shape
        rows = r // n_steps
        assert rows * n_steps == r and rows % BF16_ROWS == 0, (w.shape, n_steps)
        in_specs.append(pl.BlockSpec((1, rows, c), lambda *g, layer=layer: (layer, step_of(*g), 0)))
        for lo, hi in cols:
            out_specs.append(pl.BlockSpec((rows, hi - lo), lambda *g: (step_of(*g), 0)))
            out_shapes.append(jax.ShapeDtypeStruct((r, hi - lo), BF16))
        splits.append(tuple(cols))
    return in_specs, out_specs, out_shapes, tuple(splits)


def _run_casts(cast_in, cast_out, splits):
    k = 0
    for src, cols in zip(cast_in, splits):
        for lo, hi in cols:
            cast_out[k][...] = src[0, :, lo:hi].astype(BF16)
            k += 1


def _modnorm(xf, gain_scale, shift):
    ms = jnp.mean(xf * xf, axis=-1, keepdims=True)
    return (xf * lax.rsqrt(ms + NORM_EPS)) * gain_scale + shift


def _gelu_tanh(x):
    c0 = -2.0 * math.sqrt(2.0 / math.pi) * LOG2_E
    c1 = c0 * 0.044715
    return x / (1.0 + jnp.exp2(x * (c0 + c1 * (x * x))))


def _mod_kernel(c_ref, w_ref, b_ref, o_ref):
    c = c_ref[...]
    s = (c * jax.nn.sigmoid(c)).astype(BF16)
    o_ref[0] = jnp.dot(s, w_ref[0].astype(BF16), preferred_element_type=F32) + b_ref[0]


def _modulation(cc, w_mod, b_mod):
    depth, d, n = w_mod.shape
    tn = 1024
    return pl.pallas_call(
        _mod_kernel,
        grid=(depth, n // tn),
        in_specs=[
            pl.BlockSpec((MOD_ROWS, d), lambda l, j: (0, 0)),
            pl.BlockSpec((1, d, tn), lambda l, j: (l, 0, j)),
            pl.BlockSpec((1, 1, tn), lambda l, j: (l, 0, j)),
        ],
        out_specs=pl.BlockSpec((1, MOD_ROWS, tn), lambda l, j: (l, 0, j)),
        out_shape=jax.ShapeDtypeStruct((depth, MOD_ROWS, n), F32),
        compiler_params=_params(2),
        name="modulation",
    )(cc, w_mod, b_mod.reshape(depth, 1, n))


def _inproj_kernel(*refs, tm, seq, rope, d_pool, d_attn, d_kv, d_sg):
    if rope:
        (xp_ref, x_ref, xn_ref, mod_ref, g1_ref, wf_ref, wqvk_ref, wpool_ref, pscale_ref, sgg_ref, wsp_ref, bsp_ref,
         cos_q_ref, sin_q_ref, cos_k_ref, sin_k_ref, ypool_ref, qt_ref, k_ref, vt_ref, ysg_ref) = refs
    else:
        (xp_ref, x_ref, xn_ref, mod_ref, g1_ref, wf_ref, wqvk_ref, wpool_ref, pscale_ref, sgg_ref, wsp_ref, bsp_ref,
         ypool_ref, qt_ref, k_ref, vt_ref, ysg_ref) = refs
    n_ext = tm + 2 * HALO
    tiles_per_seq = seq // tm
    t_in_seq = lax.rem(pl.program_id(0), tiles_per_seq)
    start = t_in_seq * tm

    shift = mod_ref[0, 0:1, :]
    gain_scale = g1_ref[...] * (1.0 + mod_ref[0, 1:2, :])
    h_bf = _modnorm(x_ref[...], gain_scale, shift).astype(BF16)
    h_ext = jnp.concatenate([_modnorm(xp_ref[...], gain_scale, shift).astype(BF16), h_bf,
                             _modnorm(xn_ref[...], gain_scale, shift).astype(BF16)], axis=0)
    off_u = d_pool + d_attn + 2 * d_kv
    zp = jnp.dot(h_ext, wf_ref[:, :d_pool], preferred_element_type=F32)
    zr = jnp.dot(h_bf, wf_ref[:, off_u:off_u + 2 * d_sg], preferred_element_type=F32)
    zt = lax.dot_general(wqvk_ref[...], h_bf, NT_DIMS, preferred_element_type=F32)

    halo_row = lax.broadcasted_iota(jnp.int32, (HALO, LANES), 0)
    keep_head = halo_row >= jnp.where(t_in_seq == 0, HALO, 0)
    keep_tail = halo_row < jnp.where(t_in_seq == tiles_per_seq - 1, 0, HALO)
    edge = SUBLANES
    edge_row = lax.broadcasted_iota(jnp.int32, (edge, LANES), 0)
    pos_head = edge_row + start
    pos_tail = edge_row + (start + tm - edge)

    def shifted(a, s):
        return pltpu.roll(a, s % n_ext, 0)

    def inv_count(pos, half):
        return 1.0 / (jnp.minimum(pos + half, seq) - jnp.maximum(pos - half, 0)).astype(F32)

    pooled = []
    for g, w in enumerate(POOL_WINDOWS):
        half = w // 2
        cols = slice(g * LANES, (g + 1) * LANES)
        e = jnp.concatenate([jnp.where(keep_head, zp[:HALO, cols], 0.0), zp[HALO:HALO + tm, cols],
                             jnp.where(keep_tail, zp[HALO + tm:, cols], 0.0)], axis=0)
        trail = e
        s = 1
        while s < half:
            trail = trail + shifted(trail, s)
            s *= 2
        wsum = shifted(trail, -(half - 1)) + shifted(trail, 1) if half > 1 else trail + shifted(trail, 1)
        ws = wsum[HALO:HALO + tm]
        own = e[HALO:HALO + tm]
        pooled.append(jnp.concatenate([
            ws[:edge] * inv_count(pos_head, half) - own[:edge],
            ws[edge:tm - edge] * (1.0 / w) - own[edge:tm - edge],
            ws[tm - edge:] * inv_count(pos_tail, half) - own[tm - edge:]], axis=0).astype(BF16))
    for pair in range(len(POOL_WINDOWS) // 2):
        cols = slice(2 * pair * LANES, 2 * (pair + 1) * LANES)
        mixed = jnp.dot(jnp.concatenate(pooled[2 * pair:2 * pair + 2], axis=1), wpool_ref[pair],
                        preferred_element_type=F32)
        ypool_ref[:, cols] = (mixed * pscale_ref[:, cols]).astype(BF16)

    quarter = HEAD_DIM // 4

    def head_t(i):
        return zt[i * HEAD_DIM:(i + 1) * HEAD_DIM]

    def rotate_t(blk, cos_ref, sin_ref):
        swapped = jnp.concatenate([blk[quarter:2 * quarter], blk[:quarter],
                                   blk[3 * quarter:], blk[2 * quarter:3 * quarter]], axis=0)
        return blk * cos_ref[...] + swapped * sin_ref[...]

    n_q, n_kv = d_attn // HEAD_DIM, d_kv // HEAD_DIM
    for hq in range(n_q):
        blk = head_t(hq)
        qt_ref[hq * HEAD_DIM:(hq + 1) * HEAD_DIM, :] = (rotate_t(blk, cos_q_ref, sin_q_ref) if rope else blk).astype(BF16)
    vt_ref[...] = zt[d_attn:d_attn + d_kv].astype(BF16)
    k_heads = [head_t(n_q + n_kv + hk) for hk in range(n_kv)]
    k_heads = [rotate_t(blk, cos_k_ref, sin_k_ref) if rope else blk * Q_SCALE for blk in k_heads]
    k_ref[...] = jnp.concatenate(k_heads, axis=0).T.astype(BF16)

    gu = _gelu_tanh(zr[:, :d_sg])
    gs = _gelu_tanh(zr[:, d_sg:])
    ms = jnp.mean(gs * gs, axis=-1, keepdims=True)
    vn = ((gs * lax.rsqrt(ms + NORM_EPS)) * sgg_ref[...]).astype(BF16)
    chunks = [slice(c * CHUNK, (c + 1) * CHUNK) for c in range(tm // CHUNK)]
    for g in range(N_SG_GROUPS):
        cols = slice(g * LANES, (g + 1) * LANES)
        mixed = jnp.dot(wsp_ref[g], jnp.concatenate([vn[r, cols] for r in chunks], axis=1),
                        preferred_element_type=F32)
        for r in chunks:
            ysg_ref[r, cols] = (gu[r, cols] * (mixed[:, r] + bsp_ref[:, cols])).astype(BF16)


def _inproj(x2, mod_l, mod_row_fn, g1, wf, wqv, wpool, pscale, sgg, wsp, bsp, rope_tabs, *, tm, seq, dims):
    t, d = x2.shape
    d_pool, d_attn, d_kv, d_sg = dims
    n_tiles = t // tm
    assert max(POOL_WINDOWS) // 2 <= SUBLANES <= HALO and tm % HALO == 0
    hb = tm // HALO
    n_hblk = t // HALO
    tiles_per_seq = seq // tm
    rope = rope_tabs is not None
    in_specs = [
        pl.BlockSpec((HALO, d), lambda i: (jnp.maximum(i * hb - 1, 0), 0)),
        pl.BlockSpec((tm, d), lambda i: (i, 0)),
        pl.BlockSpec((HALO, d), lambda i: (jnp.minimum((i + 1) * hb, n_hblk - 1), 0)),
        pl.BlockSpec((1, N_MOD, d), lambda i: (mod_row_fn(i), 0, 0)),
        _const_spec(g1.shape), _const_spec(wf.shape), _const_spec(wqv.shape), _const_spec(wpool.shape),
        _const_spec(pscale.shape), _const_spec(sgg.shape), _const_spec(wsp.shape), _const_spec(bsp.shape),
    ]
    args = [x2, x2, x2, mod_l, g1, wf, wqv, wpool, pscale, sgg, wsp, bsp]
    if rope:
        in_specs += [pl.BlockSpec((HEAD_DIM, tm), lambda i: (0, lax.rem(i, tiles_per_seq)))] * len(rope_tabs)
        args += list(rope_tabs)
    row = lambda i: (i, 0)
    col = lambda i: (0, i)
    return pl.pallas_call(
        functools.partial(_inproj_kernel, tm=tm, seq=seq, rope=rope, d_pool=d_pool, d_attn=d_attn,
                          d_kv=d_kv, d_sg=d_sg),
        grid=(n_tiles,),
        in_specs=in_specs,
        out_specs=[pl.BlockSpec((tm, d_pool), row), pl.BlockSpec((d_attn, tm), col), pl.BlockSpec((tm, d_kv), row),
                   pl.BlockSpec((d_kv, tm), col), pl.BlockSpec((tm, d_sg), row)],
        out_shape=[jax.ShapeDtypeStruct((t, d_pool), BF16), jax.ShapeDtypeStruct((d_attn, t), BF16),
                   jax.ShapeDtypeStruct((t, d_kv), BF16), jax.ShapeDtypeStruct((d_kv, t), BF16),
                   jax.ShapeDtypeStruct((t, d_sg), BF16)],
        compiler_params=_params(1),
        name="inproj_rope" if rope else "inproj_ctx",
    )(*args)


def _logits(q_rhs, k_cat, masks, sink_row):
    s = jnp.dot(k_cat, q_rhs, preferred_element_type=F32)
    if masks:
        pieces, at = [], 0
        for rs, mk in masks:
            if rs.start > at:
                pieces.append(s[at:rs.start])
            pieces.append(jnp.where(mk, s[rs], MASK_VALUE))
            at = rs.stop
        if at < s.shape[0]:
            pieces.append(s[at:])
        s = jnp.concatenate(pieces, axis=0)
    return s, jnp.maximum(jnp.max(s, axis=0, keepdims=True), sink_row)


def _probabilities(s, m, sink_row):
    p = jnp.exp2(s - m)
    denom = jnp.sum(p, axis=0, keepdims=True) + jnp.exp2(sink_row - m)
    return p.astype(BF16), denom


def _weighted_values(p, denom, vt_cat):
    return jnp.dot(vt_cat, p, preferred_element_type=F32) / denom


def _q_rhs(qt_ref, h, cols):
    blocks = [qt_ref[(GQA_GROUP * h + j) * HEAD_DIM:(GQA_GROUP * h + j + 1) * HEAD_DIM, cols]
              for j in range(GQA_GROUP)]
    q = jnp.concatenate(blocks, axis=1)
    z = jnp.zeros_like(q)
    return jnp.concatenate([q, z] if h == 0 else [z, q], axis=0)


def _store_heads(o_ref, h, rows, out):
    for j in range(0, GQA_GROUP, 2):
        pair = jnp.concatenate([out[:, j * ATTN_BLOCK:(j + 1) * ATTN_BLOCK],
                                out[:, (j + 1) * ATTN_BLOCK:(j + 2) * ATTN_BLOCK]], axis=0)
        lo = (GQA_GROUP * h + j) * HEAD_DIM
        o_ref[rows, lo:lo + 2 * HEAD_DIM] = pair.T.astype(BF16)


def _win_attn_kernel(*refs, qt, n_steps, casts):
    n_in = 10
    qt_ref, kp_ref, ko_ref, kn_ref, vp_ref, vo_ref, vn_ref, kc_ref, vc_ref, sink_ref = refs[:n_in]
    cast_in = refs[n_in:n_in + len(casts)]
    o_ref = refs[n_in + len(casts)]
    _run_casts(cast_in, refs[n_in + len(casts) + 1:], casts)
    step = pl.program_id(1)
    nsub = qt // ATTN_BLOCK
    key = lax.broadcasted_iota(jnp.int32, (ATTN_BLOCK, QCOLS), 0)
    qry = lax.rem(lax.broadcasted_iota(jnp.int32, (ATTN_BLOCK, QCOLS), 1), ATTN_BLOCK)
    tri_prev = key >= qry
    tri_next = key <= qry
    edge_prev = key >= qry + jnp.where(step > 0, 0, 2 * ATTN_BLOCK)
    edge_next = key <= qry - jnp.where(step < n_steps - 1, 0, 2 * ATTN_BLOCK)
    prev_rows = slice(0, ATTN_BLOCK)
    next_rows = slice(2 * ATTN_BLOCK, 3 * ATTN_BLOCK)
    sink_rows = [sink_ref[h] * LOG2_E for h in range(2)]

    def logits_stage(h, n):
        r = slice(n * ATTN_BLOCK, (n + 1) * ATTN_BLOCK)
        if n > 0:
            k_prev, m_prev = ko_ref[(n - 1) * ATTN_BLOCK:n * ATTN_BLOCK, :], tri_prev
        else:
            k_prev, m_prev = kp_ref[...], edge_prev
        if n < nsub - 1:
            k_next, m_next = ko_ref[(n + 1) * ATTN_BLOCK:(n + 2) * ATTN_BLOCK, :], tri_next
        else:
            k_next, m_next = kn_ref[...], edge_next
        k_cat = jnp.concatenate([k_prev, ko_ref[r, :], k_next, kc_ref[...]], axis=0)
        return _logits(_q_rhs(qt_ref, h, r), k_cat, [(prev_rows, m_prev), (next_rows, m_next)], sink_rows[h])

    def output_stage(h, n, p, denom):
        feat = slice(h * HEAD_DIM, (h + 1) * HEAD_DIM)
        r = slice(n * ATTN_BLOCK, (n + 1) * ATTN_BLOCK)
        v_prev = vo_ref[feat, (n - 1) * ATTN_BLOCK:n * ATTN_BLOCK] if n > 0 else vp_ref[feat, :]
        v_next = vo_ref[feat, (n + 1) * ATTN_BLOCK:(n + 2) * ATTN_BLOCK] if n < nsub - 1 else vn_ref[feat, :]
        vt_cat = jnp.concatenate([v_prev, vo_ref[feat, r], v_next, vc_ref[feat, :]], axis=1)
        _store_heads(o_ref, h, r, _weighted_values(p, denom, vt_cat))

    work = [(h, n) for h in range(2) for n in range(nsub)]
    logits, probs = {}, {}
    for i in range(len(work) + 2):
        if i < len(work):
            logits[i] = logits_stage(*work[i])
        if 0 <= i - 1 < len(work):
            probs[i - 1] = _probabilities(*logits.pop(i - 1), sink_rows[work[i - 1][0]])
        if 0 <= i - 2 < len(work):
            output_stage(*work[i - 2], *probs.pop(i - 2))


def _window_attention(q_t, k, v_t, kc, vc_t, sink_tab, casts, *, n_batch, seq, ctx_len, qt):
    dq, t = q_t.shape
    dkv = k.shape[1]
    n_steps = seq // qt
    sub = qt // ATTN_BLOCK
    blocks_per_seq = seq // ATTN_BLOCK
    own = lambda b, j: b * n_steps + j
    prev = lambda b, j: b * blocks_per_seq + jnp.maximum(j * sub - 1, 0)
    nxt = lambda b, j: b * blocks_per_seq + jnp.minimum((j + 1) * sub, blocks_per_seq - 1)
    rows = lambda f: (lambda b, j: (f(b, j), 0))
    cols = lambda f: (lambda b, j: (0, f(b, j)))
    cast_in, cast_out, cast_shapes, splits = _cast_plumbing(casts, n_batch * n_steps, own)
    return pl.pallas_call(
        functools.partial(_win_attn_kernel, qt=qt, n_steps=n_steps, casts=splits),
        grid=(n_batch, n_steps),
        in_specs=[
            pl.BlockSpec((dq, qt), cols(own)),
            pl.BlockSpec((ATTN_BLOCK, dkv), rows(prev)), pl.BlockSpec((qt, dkv), rows(own)),
            pl.BlockSpec((ATTN_BLOCK, dkv), rows(nxt)),
            pl.BlockSpec((dkv, ATTN_BLOCK), cols(prev)), pl.BlockSpec((dkv, qt), cols(own)),
            pl.BlockSpec((dkv, ATTN_BLOCK), cols(nxt)),
            pl.BlockSpec((ctx_len, dkv), lambda b, j: (b, 0)), pl.BlockSpec((dkv, ctx_len), lambda b, j: (0, b)),
            pl.BlockSpec(sink_tab.shape, lambda b, j: (0, 0, 0)),
        ] + cast_in,
        out_specs=[pl.BlockSpec((qt, dq), rows(own))] + cast_out,
        out_shape=[jax.ShapeDtypeStruct((t, dq), BF16)] + cast_shapes,
        compiler_params=_params(2),
        name="window_attention",
    )(q_t, k, k, k, v_t, v_t, v_t, kc, vc_t, sink_tab, *[w for w, _, _ in casts])


def _ctx_attn_kernel(qt_ref, k_ref, vt_ref, sink_ref, o_ref, *, ctx_len):
    sink_rows = [sink_ref[h] * LOG2_E for h in range(2)]
    blocks = [slice(n * ATTN_BLOCK, (n + 1) * ATTN_BLOCK) for n in range(ctx_len // ATTN_BLOCK)]
    work = [(h, r) for h in range(2) for r in blocks]
    logits, probs = {}, {}
    for i in range(len(work) + 2):
        if i < len(work):
            h, r = work[i]
            logits[i] = _logits(_q_rhs(qt_ref, h, r), k_ref[...], [], sink_rows[h])
        if 0 <= i - 1 < len(work):
            probs[i - 1] = _probabilities(*logits.pop(i - 1), sink_rows[work[i - 1][0]])
        if 0 <= i - 2 < len(work):
            h, r = work[i - 2]
            out = _weighted_values(*probs.pop(i - 2), vt_ref[h * HEAD_DIM:(h + 1) * HEAD_DIM, :])
            _store_heads(o_ref, h, r, out)


def _context_attention(q_t, k, v_t, sink_tab, *, n_batch, ctx_len):
    dq, t = q_t.shape
    dkv = k.shape[1]
    return pl.pallas_call(
        functools.partial(_ctx_attn_kernel, ctx_len=ctx_len),
        grid=(n_batch,),
        in_specs=[pl.BlockSpec((dq, ctx_len), lambda b: (0, b)), pl.BlockSpec((ctx_len, dkv), lambda b: (b, 0)),
                  pl.BlockSpec((dkv, ctx_len), lambda b: (0, b)), pl.BlockSpec(sink_tab.shape, lambda b: (0, 0, 0))],
        out_specs=pl.BlockSpec((ctx_len, dq), lambda b: (b, 0)),
        out_shape=jax.ShapeDtypeStruct((t, dq), BF16),
        compiler_params=_params(1),
        name="context_attention",
    )(q_t, k, v_t, sink_tab)


def _row_parts(tm, sub):
    return [slice(r, r + sub) for r in range(0, tm, sub)]


def _merge_kernel(*refs, sub, casts):
    n_in = 11
    x_ref, mod_ref, g1_ref, yp_ref, ya_ref, ys_ref, wg_ref, wbp_ref, wba_ref, wbs_ref, wo_ref = refs[:n_in]
    o_ref = refs[n_in + len(casts)]
    _run_casts(refs[n_in:n_in + len(casts)], refs[n_in + len(casts) + 1:], casts)
    tm, d = x_ref.shape
    shift, gate = mod_ref[0, 0:1, :], mod_ref[0, 2:3, :]
    gain_scale = g1_ref[...] * (1.0 + mod_ref[0, 1:2, :])
    for r in _row_parts(tm, sub):
        x = x_ref[r, :]
        hx = _modnorm(x, gain_scale, shift).astype(BF16)
        y = None
        for b, (y_ref, w_ref) in enumerate(((yp_ref, wbp_ref), (ya_ref, wba_ref), (ys_ref, wbs_ref))):
            gate_b = jax.nn.sigmoid(jnp.dot(hx, wg_ref[:, b * d:(b + 1) * d], preferred_element_type=F32))
            part = gate_b * jnp.dot(y_ref[r, :], w_ref[...], preferred_element_type=F32)
            y = part if y is None else y + part
        o = jnp.dot(y.astype(BF16), wo_ref[...], preferred_element_type=F32)
        o_ref[r, :] = x + gate * o


def _merge(x2, mod_l, mod_row_fn, g1, yp, ya, ys, wg, wbp, wba, wbs, wo, casts, *, tm, sub):
    t, d = x2.shape
    row = lambda i: (i, 0)
    cast_in, cast_out, cast_shapes, splits = _cast_plumbing(casts, t // tm, lambda i: i)
    return pl.pallas_call(
        functools.partial(_merge_kernel, sub=sub, casts=splits),
        grid=(t // tm,),
        in_specs=[
            pl.BlockSpec((tm, d), row),
            pl.BlockSpec((1, N_MOD, d), lambda i: (mod_row_fn(i), 0, 0)),
            _const_spec(g1.shape),
            pl.BlockSpec((tm, yp.shape[1]), row), pl.BlockSpec((tm, ya.shape[1]), row),
            pl.BlockSpec((tm, ys.shape[1]), row),
            _const_spec(wg.shape), _const_spec(wbp.shape), _const_spec(wba.shape), _const_spec(wbs.shape),
            _const_spec(wo.shape),
        ] + cast_in,
        out_specs=[pl.BlockSpec((tm, d), row)] + cast_out,
        out_shape=[jax.ShapeDtypeStruct((t, d), F32)] + cast_shapes,
        compiler_params=_params(1),
        name="merge",
    )(x2, mod_l, g1, yp, ya, ys, wg, wbp, wba, wbs, wo, *[w for w, _, _ in casts])


def _ffn_chunks(d_ff):
    n_tiles = d_ff // MXU_DIM
    first = (n_tiles + 1) // 2 * MXU_DIM
    return ((0, first), (first, d_ff))


def _ffn_kernel(*refs, d_ff, final, sub, casts):
    n_in = 6 if final else 5
    x_ref, mod_ref, g2_ref, wi_ref, wo_ref = refs[:5]
    fg_ref = refs[5] if final else None
    o_ref = refs[n_in + len(casts)]
    _run_casts(refs[n_in:n_in + len(casts)], refs[n_in + len(casts) + 1:], casts)
    shift, gate = mod_ref[0, 3:4, :], mod_ref[0, 5:6, :]
    gain_scale = g2_ref[...] * (1.0 + mod_ref[0, 4:5, :])
    for r in _row_parts(x_ref.shape[0], sub):
        x = x_ref[r, :]
        h = _modnorm(x, gain_scale, shift).astype(BF16)
        acc = None
        for lo, hi in _ffn_chunks(d_ff):
            a = jnp.dot(h, wi_ref[:, lo:hi], preferred_element_type=F32)
            b = jnp.dot(h, wi_ref[:, d_ff + lo:d_ff + hi], preferred_element_type=F32)
            act = ((a * jax.nn.sigmoid(a)) * b).astype(BF16)
            part = jnp.dot(act, wo_ref[lo:hi, :], preferred_element_type=F32)
            acc = part if acc is None else acc + part
        y = x + gate * acc
        if final:
            ms = jnp.mean(y * y, axis=-1, keepdims=True)
            y = (y * lax.rsqrt(ms + NORM_EPS)) * fg_ref[...]
        o_ref[r, :] = y


def _ffn(x2, mod_l, mod_row_fn, g2, wi, wo, final_gain, casts, *, tm, sub):
    t, d = x2.shape
    d_ff = wo.shape[0]
    final = final_gain is not None
    row = lambda i: (i, 0)
    in_specs = [
        pl.BlockSpec((tm, d), row),
        pl.BlockSpec((1, N_MOD, d), lambda i: (mod_row_fn(i), 0, 0)),
        _const_spec(g2.shape), _const_spec(wi.shape), _const_spec(wo.shape),
    ]
    args = [x2, mod_l, g2, wi, wo]
    if final:
        in_specs.append(_const_spec(final_gain.shape))
        args.append(final_gain)
    cast_in, cast_out, cast_shapes, splits = _cast_plumbing(casts, t // tm, lambda i: i)
    return pl.pallas_call(
        functools.partial(_ffn_kernel, d_ff=d_ff, final=final, sub=sub, casts=splits),
        grid=(t // tm,),
        in_specs=in_specs + cast_in,
        out_specs=[pl.BlockSpec((tm, d), row)] + cast_out,
        out_shape=[jax.ShapeDtypeStruct((t, d), F32)] + cast_shapes,
        compiler_params=_params(1),
        name="ffn_final" if final else "ffn",
    )(*args, *[w for w, _, _ in casts])


def _rope_tables(seq):
    rows = seq // GRID_W
    freqs = HEAD_DIM // 4
    row = jnp.repeat(jnp.arange(rows), GRID_W).astype(F32)
    col = jnp.tile(jnp.arange(GRID_W), rows).astype(F32)
    inv_freq = ROPE_THETA ** (-jnp.arange(freqs, dtype=F32) / freqs)
    ang_r = row[:, None] * inv_freq[None, :]
    ang_c = col[:, None] * inv_freq[None, :]
    cos = jnp.concatenate([jnp.cos(ang_r), jnp.cos(ang_r), jnp.cos(ang_c), jnp.cos(ang_c)], axis=-1)
    sin = jnp.concatenate([-jnp.sin(ang_r), jnp.sin(ang_r), -jnp.sin(ang_c), jnp.sin(ang_c)], axis=-1)
    return cos.T, sin.T, cos.T * Q_SCALE, sin.T * Q_SCALE


def _pick_tile(seq, target):
    tm = min(seq, target)
    assert seq % tm == 0 and tm % CHUNK == 0
    return tm


def kernel(x, c, ctx, c_ctx, w_mod, b_mod, norm1_gain, norm2_gain, w_in, w_pool, pool_scale, attn_sink, sg_v_gain,
           w_spatial, b_spatial, w_br_pool, w_br_attn, w_br_sg, w_out, w_ffn_in, w_ffn_out, final_gain):
    n_batch, seq, d = x.shape
    ctx_len = ctx.shape[1]
    depth = w_mod.shape[0]
    d_pool = w_br_pool.shape[1]
    d_attn = w_br_attn.shape[1]
    d_sg = w_br_sg.shape[1]
    n_q_heads = attn_sink.shape[1]
    d_kv = (n_q_heads // GQA_GROUP) * HEAD_DIM
    off_k = d_pool + d_attn
    off_v = off_k + d_kv
    off_u = off_v + d_kv
    off_gate = off_u + 2 * d_sg
    assert n_q_heads == 2 * GQA_GROUP and d_kv == LANES
    assert d_pool == len(POOL_WINDOWS) * LANES and d_sg == N_SG_GROUPS * LANES
    assert w_in.shape[2] == off_gate + N_BRANCHES * d and n_batch + 1 <= MOD_ROWS
    assert seq % GRID_W == 0 and w_spatial.shape[-1] == CHUNK
    dims = (d_pool, d_attn, d_kv, d_sg)

    tm_x = _pick_tile(seq, 512)
    tm_c = _pick_tile(ctx_len, 512)
    qt = _pick_tile(seq, 512)
    big_x = _pick_tile(seq, 1024)
    big_c = _pick_tile(n_batch * ctx_len, 1024)
    sub_x, sub_c = min(big_x, 512), min(big_c, 512)

    cc = jnp.concatenate([c, c_ctx[None, :], jnp.zeros((MOD_ROWS - n_batch - 1, d), F32)], axis=0)
    mod = _modulation(cc, w_mod, b_mod).reshape(depth, MOD_ROWS, N_MOD, d)

    rope_tabs = _rope_tables(seq)
    x_tiles_per_seq = seq // tm_x
    big_tiles_per_seq = seq // big_x
    mod_row_x = lambda i: i // x_tiles_per_seq
    mod_row_big = lambda i: i // big_tiles_per_seq
    mod_row_c = lambda i: n_batch

    xs = x.reshape(n_batch * seq, d)
    cs = ctx.reshape(n_batch * ctx_len, d)
    row2 = lambda a: a.reshape(1, -1)

    all_cols = lambda w: ((0, w.shape[2]),)
    wf, wg = w_in[0, :, :off_gate].astype(BF16), w_in[0, :, off_gate:].astype(BF16)

    for l in range(depth):
        last = l == depth - 1
        mod_l = mod[l]
        g1, g2 = row2(norm1_gain[l]), row2(norm2_gain[l])
        w_l = w_in[l]
        wqv = jnp.concatenate([w_l[:, d_pool:off_k], w_l[:, off_v:off_u], w_l[:, off_k:off_v]], axis=1).T.astype(BF16)
        zblk = jnp.zeros((LANES, LANES), F32)
        wpool = jnp.stack([jnp.block([[w_pool[l, 2 * p], zblk], [zblk, w_pool[l, 2 * p + 1]]])
                           for p in range(len(POOL_WINDOWS) // 2)]).astype(BF16)
        wsp = w_spatial[l].astype(BF16)
        pscale, sgg = row2(pool_scale[l]), row2(sg_v_gain[l])
        bsp = jnp.repeat(b_spatial[l].T, LANES, axis=1)
        sink_tab = jnp.repeat(attn_sink[l].reshape(2, GQA_GROUP), ATTN_BLOCK, axis=1)[:, None, :]

        common = (g1, wf, wqv, wpool, pscale, sgg, wsp, bsp)
        ypc, qc_t, kc, vc_t, ysc = _inproj(cs, mod_l, mod_row_c, *common, None, tm=tm_c, seq=ctx_len, dims=dims)
        ypx, qx_t, kx, vx_t, ysx = _inproj(xs, mod_l, mod_row_x, *common, rope_tabs, tm=tm_x, seq=seq, dims=dims)
        merge_casts = [(w, l, all_cols(w)) for w in (w_br_pool, w_br_attn, w_br_sg, w_out)]
        ax, wbp, wba, wbs, wo = _window_attention(qx_t, kx, vx_t, kc, vc_t, sink_tab, merge_casts,
                                                  n_batch=n_batch, seq=seq, ctx_len=ctx_len, qt=qt)
        ffn_casts = [(w, l, all_cols(w)) for w in (w_ffn_in, w_ffn_out)]
        xs, wfi, wfo = _merge(xs, mod_l, mod_row_big, g1, ypx, ax, ysx, wg, wbp, wba, wbs, wo, ffn_casts,
                              tm=big_x, sub=sub_x)
        if last:
            xs, = _ffn(xs, mod_l, mod_row_big, g2, wfi, wfo, row2(final_gain), [], tm=big_x, sub=sub_x)
        else:
            ac = _context_attention(qc_t, kc, vc_t, sink_tab, n_batch=n_batch, ctx_len=ctx_len)
            cs, = _merge(cs, mod_l, mod_row_c, g1, ypc, ac, ysc, wg, wbp, wba, wbs, wo, [], tm=big_c, sub=sub_c)
            cs, = _ffn(cs, mod_l, mod_row_c, g2, wfi, wfo, None, [], tm=big_c, sub=sub_c)
            next_casts = [(w_in, l + 1, ((0, off_gate), (off_gate, w_in.shape[2])))]
            xs, wf, wg = _ffn(xs, mod_l, mod_row_big, g2, wfi, wfo, None, next_casts, tm=big_x, sub=sub_x)
    return xs.reshape(n_batch, seq, d)
```

```python
import functools
import math

import jax
import jax.numpy as jnp
from jax import lax
from jax.experimental import pallas as pl
from jax.experimental.pallas import tpu as pltpu

F32 = jnp.float32
BF16 = jnp.bfloat16

GRID_W = 64
NORM_EPS = 1e-6
MASK_VALUE = -1e30
N_MOD = 6
POOL_WINDOWS = (2, 4, 8, 16)
HEAD_DIM = 64
GQA_GROUP = 4
ATTN_BLOCK = 128
ROPE_THETA = 10000.0
CHUNK = 128
N_SG_GROUPS = 4
N_BRANCHES = 3
LOG2_E = math.log2(math.e)

LANES = 128
SUBLANES = 8
MXU_DIM = 256
VMEM_LIMIT_BYTES = 56 * 1024 * 1024

BF16_ROWS = 2 * SUBLANES
MOD_ROWS = BF16_ROWS
HALO = BF16_ROWS
Q_SCALE = HEAD_DIM ** -0.5 * LOG2_E
QCOLS = GQA_GROUP * ATTN_BLOCK

NT_DIMS = (((1,), (1,)), ((), ()))


def _const_spec(shape):
    nd = len(shape)
    return pl.BlockSpec(shape, lambda *_: (0,) * nd, pipeline_mode=pl.Buffered(1))


def _params(n_axes=1):
    return pltpu.CompilerParams(dimension_semantics=("arbitrary",) * n_axes,
                                vmem_limit_bytes=VMEM_LIMIT_BYTES)


def _cast_plumbing(casts, n_steps, step_of):
    in_specs, out_specs, out_shapes, splits = [], [], [], []
    for w, layer, cols in casts:
        _, r, c = w.shape
        rows = r // n_steps
        assert rows * n_steps == r and rows % BF16_ROWS == 0, (w.shape, n_steps)
        in_specs.append(pl.BlockSpec((1, rows, c), lambda *g, layer=layer: (layer, step_of(*g), 0)))
        for lo, hi in cols:
            out_specs.append(pl.BlockSpec((rows, hi - lo), lambda *g: (step_of(*g), 0)))
            out_shapes.append(jax.ShapeDtypeStruct((r, hi - lo), BF16))
        splits.append(tuple(cols))
    return in_specs, out_specs, out_shapes, tuple(splits)


def _run_casts(cast_in, cast_out, splits):
    k = 0
    for src, cols in zip(cast_in, splits):
        for lo, hi in cols:
            cast_out[k][...] = src[0, :, lo:hi].astype(BF16)
            k += 1


def _modnorm(xf, gain_scale, shift):
    ms = jnp.mean(xf * xf, axis=-1, keepdims=True)
    return (xf * lax.rsqrt(ms + NORM_EPS)) * gain_scale + shift


def _gelu_tanh(x):
    c0 = -2.0 * math.sqrt(2.0 / math.pi) * LOG2_E
    c1 = c0 * 0.044715
    return x / (1.0 + jnp.exp2(x * (c0 + c1 * (x * x))))


def _mod_kernel(c_ref, w_ref, b_ref, o_ref):
    c = c_ref[...]
    s = (c * jax.nn.sigmoid(c)).astype(BF16)
    o_ref[0] = jnp.dot(s, w_ref[0].astype(BF16), preferred_element_type=F32) + b_ref[0]


def _modulation(cc, w_mod, b_mod):
    depth, d, n = w_mod.shape
    tn = 1024
    return pl.pallas_call(
        _mod_kernel,
        grid=(depth, n // tn),
        in_specs=[
            pl.BlockSpec((MOD_ROWS, d), lambda l, j: (0, 0)),
            pl.BlockSpec((1, d, tn), lambda l, j: (l, 0, j)),
            pl.BlockSpec((1, 1, tn), lambda l, j: (l, 0, j)),
        ],
        out_specs=pl.BlockSpec((1, MOD_ROWS, tn), lambda l, j: (l, 0, j)),
        out_shape=jax.ShapeDtypeStruct((depth, MOD_ROWS, n), F32),
        compiler_params=_params(2),
        name="modulation",
    )(cc, w_mod, b_mod.reshape(depth, 1, n))


def _inproj_kernel(*refs, tm, sub, seq, rope, d_pool, d_attn, d_kv, d_sg):
    if rope:
        (xp_ref, x_ref, xn_ref, mod_ref, g1_ref, wf_ref, wqvk_ref, wpool_ref, pscale_ref, sgg_ref, wsp_ref, bsp_ref,
         cos_q_ref, sin_q_ref, cos_k_ref, sin_k_ref, ypool_ref, qt_ref, k_ref, vt_ref, ysg_ref) = refs
    else:
        (xp_ref, x_ref, xn_ref, mod_ref, g1_ref, wf_ref, wqvk_ref, wpool_ref, pscale_ref, sgg_ref, wsp_ref, bsp_ref,
         ypool_ref, qt_ref, k_ref, vt_ref, ysg_ref) = refs
    n_parts = tm // sub
    n_ext = sub + 2 * HALO
    tiles_per_seq = seq // tm
    t_in_seq = lax.rem(pl.program_id(0), tiles_per_seq)
    shift = mod_ref[0, 0:1, :]
    gain_scale = g1_ref[...] * (1.0 + mod_ref[0, 1:2, :])
    off_u = d_pool + d_attn + 2 * d_kv
    halo_row = lax.broadcasted_iota(jnp.int32, (HALO, LANES), 0)
    edge = SUBLANES
    edge_row = lax.broadcasted_iota(jnp.int32, (edge, LANES), 0)
    quarter = HEAD_DIM // 4
    n_q, n_kv = d_attn // HEAD_DIM, d_kv // HEAD_DIM

    def normed(x):
        return _modnorm(x, gain_scale, shift).astype(BF16)

    def project(p):
        lo = p * sub
        x_prev = xp_ref[...] if p == 0 else x_ref[lo - HALO:lo, :]
        x_next = xn_ref[...] if p == n_parts - 1 else x_ref[lo + sub:lo + sub + HALO, :]
        h_bf = normed(x_ref[lo:lo + sub, :])
        h_ext = jnp.concatenate([normed(x_prev), h_bf, normed(x_next)], axis=0)
        zp = jnp.dot(h_ext, wf_ref[:, :d_pool], preferred_element_type=F32)
        zr = jnp.dot(h_bf, wf_ref[:, off_u:off_u + 2 * d_sg], preferred_element_type=F32)
        zt = lax.dot_general(wqvk_ref[...], h_bf, NT_DIMS, preferred_element_type=F32)
        return zt, zp, zr

    def shifted(a, s):
        return pltpu.roll(a, s % n_ext, 0)

    def inv_count(pos, half):
        return 1.0 / (jnp.minimum(pos + half, seq) - jnp.maximum(pos - half, 0)).astype(F32)

    def rotate_t(blk, cos_ref, sin_ref, tok):
        swapped = jnp.concatenate([blk[quarter:2 * quarter], blk[:quarter],
                                   blk[3 * quarter:], blk[2 * quarter:3 * quarter]], axis=0)
        return blk * cos_ref[:, tok] + swapped * sin_ref[:, tok]

    def finish(p, zt, zp, zr):
        tok = slice(p * sub, (p + 1) * sub)
        start = t_in_seq * tm + p * sub

        keep_head = halo_row >= jnp.where(t_in_seq == 0, HALO, 0) if p == 0 else None
        keep_tail = halo_row < jnp.where(t_in_seq == tiles_per_seq - 1, 0, HALO) if p == n_parts - 1 else None
        pos_head = edge_row + start
        pos_tail = edge_row + (start + sub - edge)
        pooled = []
        for g, w in enumerate(POOL_WINDOWS):
            half = w // 2
            cols = slice(g * LANES, (g + 1) * LANES)
            head, tail = zp[:HALO, cols], zp[HALO + sub:, cols]
            e = jnp.concatenate([head if keep_head is None else jnp.where(keep_head, head, 0.0),
                                 zp[HALO:HALO + sub, cols],
                                 tail if keep_tail is None else jnp.where(keep_tail, tail, 0.0)], axis=0)
            trail = e
            s = 1
            while s < half:
                trail = trail + shifted(trail, s)
                s *= 2
            wsum = shifted(trail, -(half - 1)) + shifted(trail, 1) if half > 1 else trail + shifted(trail, 1)
            ws = wsum[HALO:HALO + sub]
            own = e[HALO:HALO + sub]
            pooled.append(jnp.concatenate([
                ws[:edge] * inv_count(pos_head, half) - own[:edge],
                ws[edge:sub - edge] * (1.0 / w) - own[edge:sub - edge],
                ws[sub - edge:] * inv_count(pos_tail, half) - own[sub - edge:]], axis=0).astype(BF16))
        for pair in range(len(POOL_WINDOWS) // 2):
            cols = slice(2 * pair * LANES, 2 * (pair + 1) * LANES)
            mixed = jnp.dot(jnp.concatenate(pooled[2 * pair:2 * pair + 2], axis=1), wpool_ref[pair],
                            preferred_element_type=F32)
            ypool_ref[tok, cols] = (mixed * pscale_ref[:, cols]).astype(BF16)

        gu = _gelu_tanh(zr[:, :d_sg])
        gs = _gelu_tanh(zr[:, d_sg:])
        ms = jnp.mean(gs * gs, axis=-1, keepdims=True)
        vn = ((gs * lax.rsqrt(ms + NORM_EPS)) * sgg_ref[...]).astype(BF16)
        chunks = [slice(c * CHUNK, (c + 1) * CHUNK) for c in range(sub // CHUNK)]
        for g in range(N_SG_GROUPS):
            cols = slice(g * LANES, (g + 1) * LANES)
            mixed = jnp.dot(wsp_ref[g], jnp.concatenate([vn[r, cols] for r in chunks], axis=1),
                            preferred_element_type=F32)
            for r in chunks:
                out_rows = slice(p * sub + r.start, p * sub + r.stop)
                ysg_ref[out_rows, cols] = (gu[r, cols] * (mixed[:, r] + bsp_ref[:, cols])).astype(BF16)

        for hq in range(n_q):
            blk = zt[hq * HEAD_DIM:(hq + 1) * HEAD_DIM]
            blk = rotate_t(blk, cos_q_ref, sin_q_ref, tok) if rope else blk
            qt_ref[hq * HEAD_DIM:(hq + 1) * HEAD_DIM, tok] = blk.astype(BF16)
        vt_ref[:, tok] = zt[d_attn:d_attn + d_kv].astype(BF16)
        k_heads = [zt[(n_q + n_kv + hk) * HEAD_DIM:(n_q + n_kv + hk + 1) * HEAD_DIM] for hk in range(n_kv)]
        k_heads = [rotate_t(blk, cos_k_ref, sin_k_ref, tok) if rope else blk * Q_SCALE for blk in k_heads]
        k_ref[tok, :] = jnp.concatenate(k_heads, axis=0).T.astype(BF16)

    staged = project(0)
    for p in range(n_parts):
        ahead = project(p + 1) if p + 1 < n_parts else None
        finish(p, *staged)
        staged = ahead


def _inproj(x2, mod_l, mod_row_fn, g1, wf, wqv, wpool, pscale, sgg, wsp, bsp, rope_tabs, *, tm, sub, seq, dims):
    t, d = x2.shape
    d_pool, d_attn, d_kv, d_sg = dims
    n_tiles = t // tm
    assert max(POOL_WINDOWS) // 2 <= SUBLANES <= HALO and tm % sub == 0 and sub % CHUNK == 0
    hb = tm // HALO
    n_hblk = t // HALO
    tiles_per_seq = seq // tm
    rope = rope_tabs is not None
    in_specs = [
        pl.BlockSpec((HALO, d), lambda i: (jnp.maximum(i * hb - 1, 0), 0)),
        pl.BlockSpec((tm, d), lambda i: (i, 0)),
        pl.BlockSpec((HALO, d), lambda i: (jnp.minimum((i + 1) * hb, n_hblk - 1), 0)),
        pl.BlockSpec((1, N_MOD, d), lambda i: (mod_row_fn(i), 0, 0)),
        _const_spec(g1.shape), _const_spec(wf.shape), _const_spec(wqv.shape), _const_spec(wpool.shape),
        _const_spec(pscale.shape), _const_spec(sgg.shape), _const_spec(wsp.shape), _const_spec(bsp.shape),
    ]
    args = [x2, x2, x2, mod_l, g1, wf, wqv, wpool, pscale, sgg, wsp, bsp]
    if rope:
        in_specs += [pl.BlockSpec((HEAD_DIM, tm), lambda i: (0, lax.rem(i, tiles_per_seq)))] * len(rope_tabs)
        args += list(rope_tabs)
    row = lambda i: (i, 0)
    col = lambda i: (0, i)
    return pl.pallas_call(
        functools.partial(_inproj_kernel, tm=tm, sub=sub, seq=seq, rope=rope, d_pool=d_pool, d_attn=d_attn,
                          d_kv=d_kv, d_sg=d_sg),
        grid=(n_tiles,),
        in_specs=in_specs,
        out_specs=[pl.BlockSpec((tm, d_pool), row), pl.BlockSpec((d_attn, tm), col), pl.BlockSpec((tm, d_kv), row),
                   pl.BlockSpec((d_kv, tm), col), pl.BlockSpec((tm, d_sg), row)],
        out_shape=[jax.ShapeDtypeStruct((t, d_pool), BF16), jax.ShapeDtypeStruct((d_attn, t), BF16),
                   jax.ShapeDtypeStruct((t, d_kv), BF16), jax.ShapeDtypeStruct((d_kv, t), BF16),
                   jax.ShapeDtypeStruct((t, d_sg), BF16)],
        compiler_params=_params(1),
        name="inproj_rope" if rope else "inproj_ctx",
    )(*args)


def _logits(q_rhs, k_cat, masks, sink_row):
    s = jnp.dot(k_cat, q_rhs, preferred_element_type=F32)
    if masks:
        pieces, at = [], 0
        for rs, mk in masks:
            if rs.start > at:
                pieces.append(s[at:rs.start])
            pieces.append(jnp.where(mk, s[rs], MASK_VALUE))
            at = rs.stop
        if at < s.shape[0]:
            pieces.append(s[at:])
        s = jnp.concatenate(pieces, axis=0)
    return s, jnp.maximum(jnp.max(s, axis=0, keepdims=True), sink_row)


def _probabilities(s, m, sink_row):
    p = jnp.exp2(s - m)
    denom = jnp.sum(p, axis=0, keepdims=True) + jnp.exp2(sink_row - m)
    return p.astype(BF16), denom


def _weighted_values(p, denom, vt_cat):
    return jnp.dot(vt_cat, p, preferred_element_type=F32) / denom


def _q_rhs(qt_ref, h, cols):
    blocks = [qt_ref[(GQA_GROUP * h + j) * HEAD_DIM:(GQA_GROUP * h + j + 1) * HEAD_DIM, cols]
              for j in range(GQA_GROUP)]
    q = jnp.concatenate(blocks, axis=1)
    z = jnp.zeros_like(q)
    return jnp.concatenate([q, z] if h == 0 else [z, q], axis=0)


def _store_heads(o_ref, h, rows, out):
    for j in range(0, GQA_GROUP, 2):
        pair = jnp.concatenate([out[:, j * ATTN_BLOCK:(j + 1) * ATTN_BLOCK],
                                out[:, (j + 1) * ATTN_BLOCK:(j + 2) * ATTN_BLOCK]], axis=0)
        lo = (GQA_GROUP * h + j) * HEAD_DIM
        o_ref[rows, lo:lo + 2 * HEAD_DIM] = pair.T.astype(BF16)


def _win_attn_kernel(*refs, qt, n_steps, casts):
    n_in = 10
    qt_ref, kp_ref, ko_ref, kn_ref, vp_ref, vo_ref, vn_ref, kc_ref, vc_ref, sink_ref = refs[:n_in]
    cast_in = refs[n_in:n_in + len(casts)]
    o_ref = refs[n_in + len(casts)]
    _run_casts(cast_in, refs[n_in + len(casts) + 1:], casts)
    step = pl.program_id(1)
    nsub = qt // ATTN_BLOCK
    key = lax.broadcasted_iota(jnp.int32, (ATTN_BLOCK, QCOLS), 0)
    qry = lax.rem(lax.broadcasted_iota(jnp.int32, (ATTN_BLOCK, QCOLS), 1), ATTN_BLOCK)
    tri_prev = key >= qry
    tri_next = key <= qry
    edge_prev = key >= qry + jnp.where(step > 0, 0, 2 * ATTN_BLOCK)
    edge_next = key <= qry - jnp.where(step < n_steps - 1, 0, 2 * ATTN_BLOCK)
    prev_rows = slice(0, ATTN_BLOCK)
    next_rows = slice(2 * ATTN_BLOCK, 3 * ATTN_BLOCK)
    sink_rows = [sink_ref[h] * LOG2_E for h in range(2)]

    def logits_stage(h, n):
        r = slice(n * ATTN_BLOCK, (n + 1) * ATTN_BLOCK)
        if n > 0:
            k_prev, m_prev = ko_ref[(n - 1) * ATTN_BLOCK:n * ATTN_BLOCK, :], tri_prev
        else:
            k_prev, m_prev = kp_ref[...], edge_prev
        if n < nsub - 1:
            k_next, m_next = ko_ref[(n + 1) * ATTN_BLOCK:(n + 2) * ATTN_BLOCK, :], tri_next
        else:
            k_next, m_next = kn_ref[...], edge_next
        k_cat = jnp.concatenate([k_prev, ko_ref[r, :], k_next, kc_ref[...]], axis=0)
        return _logits(_q_rhs(qt_ref, h, r), k_cat, [(prev_rows, m_prev), (next_rows, m_next)], sink_rows[h])

    def output_stage(h, n, p, denom):
        feat = slice(h * HEAD_DIM, (h + 1) * HEAD_DIM)
        r = slice(n * ATTN_BLOCK, (n + 1) * ATTN_BLOCK)
        v_prev = vo_ref[feat, (n - 1) * ATTN_BLOCK:n * ATTN_BLOCK] if n > 0 else vp_ref[feat, :]
        v_next = vo_ref[feat, (n + 1) * ATTN_BLOCK:(n + 2) * ATTN_BLOCK] if n < nsub - 1 else vn_ref[feat, :]
        vt_cat = jnp.concatenate([v_prev, vo_ref[feat, r], v_next, vc_ref[feat, :]], axis=1)
        _store_heads(o_ref, h, r, _weighted_values(p, denom, vt_cat))

    work = [(h, n) for h in range(2) for n in range(nsub)]
    logits, probs = {}, {}
    for i in range(len(work) + 2):
        if i < len(work):
            logits[i] = logits_stage(*work[i])
        if 0 <= i - 1 < len(work):
            probs[i - 1] = _probabilities(*logits.pop(i - 1), sink_rows[work[i - 1][0]])
        if 0 <= i - 2 < len(work):
            output_stage(*work[i - 2], *probs.pop(i - 2))


def _window_attention(q_t, k, v_t, kc, vc_t, sink_tab, casts, *, n_batch, seq, ctx_len, qt):
    dq, t = q_t.shape
    dkv = k.shape[1]
    n_steps = seq // qt
    sub = qt // ATTN_BLOCK
    blocks_per_seq = seq // ATTN_BLOCK
    own = lambda b, j: b * n_steps + j
    prev = lambda b, j: b * blocks_per_seq + jnp.maximum(j * sub - 1, 0)
    nxt = lambda b, j: b * blocks_per_seq + jnp.minimum((j + 1) * sub, blocks_per_seq - 1)
    rows = lambda f: (lambda b, j: (f(b, j), 0))
    cols = lambda f: (lambda b, j: (0, f(b, j)))
    cast_in, cast_out, cast_shapes, splits = _cast_plumbing(casts, n_batch * n_steps, own)
    return pl.pallas_call(
        functools.partial(_win_attn_kernel, qt=qt, n_steps=n_steps, casts=splits),
        grid=(n_batch, n_steps),
        in_specs=[
            pl.BlockSpec((dq, qt), cols(own)),
            pl.BlockSpec((ATTN_BLOCK, dkv), rows(prev)), pl.BlockSpec((qt, dkv), rows(own)),
            pl.BlockSpec((ATTN_BLOCK, dkv), rows(nxt)),
            pl.BlockSpec((dkv, ATTN_BLOCK), cols(prev)), pl.BlockSpec((dkv, qt), cols(own)),
            pl.BlockSpec((dkv, ATTN_BLOCK), cols(nxt)),
            pl.BlockSpec((ctx_len, dkv), lambda b, j: (b, 0)), pl.BlockSpec((dkv, ctx_len), lambda b, j: (0, b)),
            pl.BlockSpec(sink_tab.shape, lambda b, j: (0, 0, 0)),
        ] + cast_in,
        out_specs=[pl.BlockSpec((qt, dq), rows(own))] + cast_out,
        out_shape=[jax.ShapeDtypeStruct((t, dq), BF16)] + cast_shapes,
        compiler_params=_params(2),
        name="window_attention",
    )(q_t, k, k, k, v_t, v_t, v_t, kc, vc_t, sink_tab, *[w for w, _, _ in casts])


def _ctx_attn_kernel(qt_ref, k_ref, vt_ref, sink_ref, o_ref, *, ctx_len):
    sink_rows = [sink_ref[h] * LOG2_E for h in range(2)]
    blocks = [slice(n * ATTN_BLOCK, (n + 1) * ATTN_BLOCK) for n in range(ctx_len // ATTN_BLOCK)]
    work = [(h, r) for h in range(2) for r in blocks]
    logits, probs = {}, {}
    for i in range(len(work) + 2):
        if i < len(work):
            h, r = work[i]
            logits[i] = _logits(_q_rhs(qt_ref, h, r), k_ref[...], [], sink_rows[h])
        if 0 <= i - 1 < len(work):
            probs[i - 1] = _probabilities(*logits.pop(i - 1), sink_rows[work[i - 1][0]])
        if 0 <= i - 2 < len(work):
            h, r = work[i - 2]
            out = _weighted_values(*probs.pop(i - 2), vt_ref[h * HEAD_DIM:(h + 1) * HEAD_DIM, :])
            _store_heads(o_ref, h, r, out)


def _context_attention(q_t, k, v_t, sink_tab, *, n_batch, ctx_len):
    dq, t = q_t.shape
    dkv = k.shape[1]
    return pl.pallas_call(
        functools.partial(_ctx_attn_kernel, ctx_len=ctx_len),
        grid=(n_batch,),
        in_specs=[pl.BlockSpec((dq, ctx_len), lambda b: (0, b)), pl.BlockSpec((ctx_len, dkv), lambda b: (b, 0)),
                  pl.BlockSpec((dkv, ctx_len), lambda b: (0, b)), pl.BlockSpec(sink_tab.shape, lambda b: (0, 0, 0))],
        out_specs=pl.BlockSpec((ctx_len, dq), lambda b: (b, 0)),
        out_shape=jax.ShapeDtypeStruct((t, dq), BF16),
        compiler_params=_params(1),
        name="context_attention",
    )(q_t, k, v_t, sink_tab)


def _row_parts(tm, sub):
    return [slice(r, r + sub) for r in range(0, tm, sub)]


def _merge_kernel(*refs, sub, casts):
    n_in = 11
    x_ref, mod_ref, g1_ref, yp_ref, ya_ref, ys_ref, wg_ref, wbp_ref, wba_ref, wbs_ref, wo_ref = refs[:n_in]
    o_ref = refs[n_in + len(casts)]
    _run_casts(refs[n_in:n_in + len(casts)], refs[n_in + len(casts) + 1:], casts)
    tm, d = x_ref.shape
    shift, gate = mod_ref[0, 0:1, :], mod_ref[0, 2:3, :]
    gain_scale = g1_ref[...] * (1.0 + mod_ref[0, 1:2, :])
    for r in _row_parts(tm, sub):
        x = x_ref[r, :]
        hx = _modnorm(x, gain_scale, shift).astype(BF16)
        y = None
        for b, (y_ref, w_ref) in enumerate(((yp_ref, wbp_ref), (ya_ref, wba_ref), (ys_ref, wbs_ref))):
            gate_b = jax.nn.sigmoid(jnp.dot(hx, wg_ref[:, b * d:(b + 1) * d], preferred_element_type=F32))
            part = gate_b * jnp.dot(y_ref[r, :], w_ref[...], preferred_element_type=F32)
            y = part if y is None else y + part
        o = jnp.dot(y.astype(BF16), wo_ref[...], preferred_element_type=F32)
        o_ref[r, :] = x + gate * o


def _merge(x2, mod_l, mod_row_fn, g1, yp, ya, ys, wg, wbp, wba, wbs, wo, casts, *, tm, sub):
    t, d = x2.shape
    row = lambda i: (i, 0)
    cast_in, cast_out, cast_shapes, splits = _cast_plumbing(casts, t // tm, lambda i: i)
    return pl.pallas_call(
        functools.partial(_merge_kernel, sub=sub, casts=splits),
        grid=(t // tm,),
        in_specs=[
            pl.BlockSpec((tm, d), row),
            pl.BlockSpec((1, N_MOD, d), lambda i: (mod_row_fn(i), 0, 0)),
            _const_spec(g1.shape),
            pl.BlockSpec((tm, yp.shape[1]), row), pl.BlockSpec((tm, ya.shape[1]), row),
            pl.BlockSpec((tm, ys.shape[1]), row),
            _const_spec(wg.shape), _const_spec(wbp.shape), _const_spec(wba.shape), _const_spec(wbs.shape),
            _const_spec(wo.shape),
        ] + cast_in,
        out_specs=[pl.BlockSpec((tm, d), row)] + cast_out,
        out_shape=[jax.ShapeDtypeStruct((t, d), F32)] + cast_shapes,
        compiler_params=_params(1),
        name="merge",
    )(x2, mod_l, g1, yp, ya, ys, wg, wbp, wba, wbs, wo, *[w for w, _, _ in casts])


def _ffn_chunks(d_ff):
    n_tiles = d_ff // MXU_DIM
    first = (n_tiles + 1) // 2 * MXU_DIM
    return ((0, first), (first, d_ff))


def _ffn_kernel(*refs, d_ff, final, sub, casts):
    n_in = 6 if final else 5
    x_ref, mod_ref, g2_ref, wi_ref, wo_ref = refs[:5]
    fg_ref = refs[5] if final else None
    o_ref = refs[n_in + len(casts)]
    _run_casts(refs[n_in:n_in + len(casts)], refs[n_in + len(casts) + 1:], casts)
    shift, gate = mod_ref[0, 3:4, :], mod_ref[0, 5:6, :]
    gain_scale = g2_ref[...] * (1.0 + mod_ref[0, 4:5, :])
    for r in _row_parts(x_ref.shape[0], sub):
        x = x_ref[r, :]
        h = _modnorm(x, gain_scale, shift).astype(BF16)
        acc = None
        for lo, hi in _ffn_chunks(d_ff):
            a = jnp.dot(h, wi_ref[:, lo:hi], preferred_element_type=F32)
            b = jnp.dot(h, wi_ref[:, d_ff + lo:d_ff + hi], preferred_element_type=F32)
            act = ((a * jax.nn.sigmoid(a)) * b).astype(BF16)
            part = jnp.dot(act, wo_ref[lo:hi, :], preferred_element_type=F32)
            acc = part if acc is None else acc + part
        y = x + gate * acc
        if final:
            ms = jnp.mean(y * y, axis=-1, keepdims=True)
            y = (y * lax.rsqrt(ms + NORM_EPS)) * fg_ref[...]
        o_ref[r, :] = y


def _ffn(x2, mod_l, mod_row_fn, g2, wi, wo, final_gain, casts, *, tm, sub):
    t, d = x2.shape
    d_ff = wo.shape[0]
    final = final_gain is not None
    row = lambda i: (i, 0)
    in_specs = [
        pl.BlockSpec((tm, d), row),
        pl.BlockSpec((1, N_MOD, d), lambda i: (mod_row_fn(i), 0, 0)),
        _const_spec(g2.shape), _const_spec(wi.shape), _const_spec(wo.shape),
    ]
    args = [x2, mod_l, g2, wi, wo]
    if final:
        in_specs.append(_const_spec(final_gain.shape))
        args.append(final_gain)
    cast_in, cast_out, cast_shapes, splits = _cast_plumbing(casts, t // tm, lambda i: i)
    return pl.pallas_call(
        functools.partial(_ffn_kernel, d_ff=d_ff, final=final, sub=sub, casts=splits),
        grid=(t // tm,),
        in_specs=in_specs + cast_in,
        out_specs=[pl.BlockSpec((tm, d), row)] + cast_out,
        out_shape=[jax.ShapeDtypeStruct((t, d), F32)] + cast_shapes,
        compiler_params=_params(1),
        name="ffn_final" if final else "ffn",
    )(*args, *[w for w, _, _ in casts])


def _rope_tables(seq):
    rows = seq // GRID_W
    freqs = HEAD_DIM // 4
    row = jnp.repeat(jnp.arange(rows), GRID_W).astype(F32)
    col = jnp.tile(jnp.arange(GRID_W), rows).astype(F32)
    inv_freq = ROPE_THETA ** (-jnp.arange(freqs, dtype=F32) / freqs)
    ang_r = row[:, None] * inv_freq[None, :]
    ang_c = col[:, None] * inv_freq[None, :]
    cos = jnp.concatenate([jnp.cos(ang_r), jnp.cos(ang_r), jnp.cos(ang_c), jnp.cos(ang_c)], axis=-1)
    sin = jnp.concatenate([-jnp.sin(ang_r), jnp.sin(ang_r), -jnp.sin(ang_c), jnp.sin(ang_c)], axis=-1)
    return cos.T, sin.T, cos.T * Q_SCALE, sin.T * Q_SCALE


def _pick_tile(seq, target):
    tm = min(seq, target)
    assert seq % tm == 0 and tm % CHUNK == 0
    return tm


def kernel(x, c, ctx, c_ctx, w_mod, b_mod, norm1_gain, norm2_gain, w_in, w_pool, pool_scale, attn_sink, sg_v_gain,
           w_spatial, b_spatial, w_br_pool, w_br_attn, w_br_sg, w_out, w_ffn_in, w_ffn_out, final_gain):
    n_batch, seq, d = x.shape
    ctx_len = ctx.shape[1]
    depth = w_mod.shape[0]
    d_pool = w_br_pool.shape[1]
    d_attn = w_br_attn.shape[1]
    d_sg = w_br_sg.shape[1]
    n_q_heads = attn_sink.shape[1]
    d_kv = (n_q_heads // GQA_GROUP) * HEAD_DIM
    off_k = d_pool + d_attn
    off_v = off_k + d_kv
    off_u = off_v + d_kv
    off_gate = off_u + 2 * d_sg
    assert n_q_heads == 2 * GQA_GROUP and d_kv == LANES
    assert d_pool == len(POOL_WINDOWS) * LANES and d_sg == N_SG_GROUPS * LANES
    assert w_in.shape[2] == off_gate + N_BRANCHES * d and n_batch + 1 <= MOD_ROWS
    assert seq % GRID_W == 0 and w_spatial.shape[-1] == CHUNK
    dims = (d_pool, d_attn, d_kv, d_sg)

    tm_c = _pick_tile(ctx_len, 512)
    qt = _pick_tile(seq, 512)
    big_x = _pick_tile(seq, 1024)
    big_c = _pick_tile(n_batch * ctx_len, 1024)
    sub_x, sub_c = min(big_x, 512), min(big_c, 512)

    cc = jnp.concatenate([c, c_ctx[None, :], jnp.zeros((MOD_ROWS - n_batch - 1, d), F32)], axis=0)
    mod = _modulation(cc, w_mod, b_mod).reshape(depth, MOD_ROWS, N_MOD, d)

    rope_tabs = _rope_tables(seq)
    big_tiles_per_seq = seq // big_x
    mod_row_big = lambda i: i // big_tiles_per_seq
    mod_row_c = lambda i: n_batch

    xs = x.reshape(n_batch * seq, d)
    cs = ctx.reshape(n_batch * ctx_len, d)
    row2 = lambda a: a.reshape(1, -1)

    all_cols = lambda w: ((0, w.shape[2]),)
    wf, wg = w_in[0, :, :off_gate].astype(BF16), w_in[0, :, off_gate:].astype(BF16)

    for l in range(depth):
        last = l == depth - 1
        mod_l = mod[l]
        g1, g2 = row2(norm1_gain[l]), row2(norm2_gain[l])
        wqv = jnp.concatenate([wf[:, d_pool:off_k], wf[:, off_v:off_u], wf[:, off_k:off_v]], axis=1).T
        zblk = jnp.zeros((LANES, LANES), F32)
        wpool = jnp.stack([jnp.block([[w_pool[l, 2 * p], zblk], [zblk, w_pool[l, 2 * p + 1]]])
                           for p in range(len(POOL_WINDOWS) // 2)]).astype(BF16)
        wsp = w_spatial[l].astype(BF16)
        pscale, sgg = row2(pool_scale[l]), row2(sg_v_gain[l])
        bsp = jnp.repeat(b_spatial[l].T, LANES, axis=1)
        sink_tab = jnp.repeat(attn_sink[l].reshape(2, GQA_GROUP), ATTN_BLOCK, axis=1)[:, None, :]

        common = (g1, wf, wqv, wpool, pscale, sgg, wsp, bsp)
        ypc, qc_t, kc, vc_t, ysc = _inproj(cs, mod_l, mod_row_c, *common, None, tm=tm_c, sub=tm_c,
                                             seq=ctx_len, dims=dims)
        ypx, qx_t, kx, vx_t, ysx = _inproj(xs, mod_l, mod_row_big, *common, rope_tabs, tm=big_x,
                                             sub=sub_x, seq=seq, dims=dims)
        merge_casts = [(w, l, all_cols(w)) for w in (w_br_pool, w_br_attn, w_br_sg, w_out)]
        ax, wbp, wba, wbs, wo = _window_attention(qx_t, kx, vx_t, kc, vc_t, sink_tab, merge_casts,
                                                  n_batch=n_batch, seq=seq, ctx_len=ctx_len, qt=qt)
        ffn_casts = [(w, l, all_cols(w)) for w in (w_ffn_in, w_ffn_out)]
        xs, wfi, wfo = _merge(xs, mod_l, mod_row_big, g1, ypx, ax, ysx, wg, wbp, wba, wbs, wo, ffn_casts,
                              tm=big_x, sub=sub_x)
        if last:
            xs, = _ffn(xs, mod_l, mod_row_big, g2, wfi, wfo, row2(final_gain), [], tm=big_x, sub=sub_x)
        else:
            ac = _context_attention(qc_t, kc, vc_t, sink_tab, n_batch=n_batch, ctx_len=ctx_len)
            cs, = _merge(cs, mod_l, mod_row_c, g1, ypc, ac, ysc, wg, wbp, wba, wbs, wo, [], tm=big_c, sub=sub_c)
            cs, = _ffn(cs, mod_l, mod_row_c, g2, wfi, wfo, None, [], tm=big_c, sub=sub_c)
            next_casts = [(w_in, l + 1, ((0, off_gate), (off_gate, w_in.shape[2])))]
            xs, wf, wg = _ffn(xs, mod_l, mod_row_big, g2, wfi, wfo, None, next_casts, tm=big_x, sub=sub_x)
    return xs.reshape(n_batch, seq, d)
```

```python
import functools
import math

import jax
import jax.numpy as jnp
from jax import lax
from jax.experimental import pallas as pl
from jax.experimental.pallas import tpu as pltpu

F32 = jnp.float32
BF16 = jnp.bfloat16

GRID_W = 64
NORM_EPS = 1e-6
MASK_VALUE = -1e30
N_MOD = 6
POOL_WINDOWS = (2, 4, 8, 16)
HEAD_DIM = 64
GQA_GROUP = 4
ATTN_BLOCK = 128
ROPE_THETA = 10000.0
CHUNK = 128
N_SG_GROUPS = 4
N_BRANCHES = 3
LOG2_E = math.log2(math.e)

LANES = 128
SUBLANES = 8
MXU_DIM = 256
VMEM_LIMIT_BYTES = 56 * 1024 * 1024

BF16_ROWS = 2 * SUBLANES
MOD_ROWS = BF16_ROWS
HALO = BF16_ROWS
Q_SCALE = HEAD_DIM ** -0.5 * LOG2_E
QCOLS = GQA_GROUP * ATTN_BLOCK

NT_DIMS = (((1,), (1,)), ((), ()))


def _const_spec(shape):
    nd = len(shape)
    return pl.BlockSpec(shape, lambda *_: (0,) * nd, pipeline_mode=pl.Buffered(1))


def _params(n_axes=1):
    return pltpu.CompilerParams(dimension_semantics=("arbitrary",) * n_axes,
                                vmem_limit_bytes=VMEM_LIMIT_BYTES)


def _cast_plumbing(casts, n_steps, step_of):
    in_specs, out_specs, out_shapes, splits = [], [], [], []
    for w, layer, cols in casts:
        _, r, c = w.shape
        rows = r // n_steps
        assert rows * n_steps == r and rows % BF16_ROWS == 0, (w.shape, n_steps)
        in_specs.append(pl.BlockSpec((1, rows, c), lambda *g, layer=layer: (layer, step_of(*g), 0)))
        for lo, hi in cols:
            out_specs.append(pl.BlockSpec((rows, hi - lo), lambda *g: (step_of(*g), 0)))
            out_shapes.append(jax.ShapeDtypeStruct((r, hi - lo), BF16))
        splits.append(tuple(cols))
    return in_specs, out_specs, out_shapes, tuple(splits)


def _run_casts(cast_in, cast_out, splits):
    k = 0
    for src, cols in zip(cast_in, splits):
        for lo, hi in cols:
            cast_out[k][...] = src[0, :, lo:hi].astype(BF16)
            k += 1


def _modnorm(xf, gain_scale, shift):
    ms = jnp.mean(xf * xf, axis=-1, keepdims=True)
    return (xf * lax.rsqrt(ms + NORM_EPS)) * gain_scale + shift


def _gelu_tanh(x):
    c0 = -2.0 * math.sqrt(2.0 / math.pi) * LOG2_E
    c1 = c0 * 0.044715
    return x / (1.0 + jnp.exp2(x * (c0 + c1 * (x * x))))


def _mod_kernel(*refs, casts):
    c_ref, w_ref, b_ref = refs[:3]
    o_ref = refs[3 + len(casts)]
    _run_casts(refs[3:3 + len(casts)], refs[3 + len(casts) + 1:], casts)
    c = c_ref[...]
    s = (c * jax.nn.sigmoid(c)).astype(BF16)
    o_ref[0] = jnp.dot(s, w_ref[0].astype(BF16), preferred_element_type=F32) + b_ref[0]


def _modulation(cc, w_mod, b_mod, casts):
    depth, d, n = w_mod.shape
    tn = 768
    n_col = n // tn
    assert n_col * tn == n
    cast_in, cast_out, cast_shapes, splits = _cast_plumbing(casts, depth * n_col, lambda l, j: l * n_col + j)
    return pl.pallas_call(
        functools.partial(_mod_kernel, casts=splits),
        grid=(depth, n_col),
        in_specs=[
            pl.BlockSpec((MOD_ROWS, d), lambda l, j: (0, 0)),
            pl.BlockSpec((1, d, tn), lambda l, j: (l, 0, j)),
            pl.BlockSpec((1, 1, tn), lambda l, j: (l, 0, j)),
        ] + cast_in,
        out_specs=[pl.BlockSpec((1, MOD_ROWS, tn), lambda l, j: (l, 0, j))] + cast_out,
        out_shape=[jax.ShapeDtypeStruct((depth, MOD_ROWS, n), F32)] + cast_shapes,
        compiler_params=_params(2),
        name="modulation",
    )(cc, w_mod, b_mod.reshape(depth, 1, n), *[w for w, _, _ in casts])


def _inproj_kernel(*refs, tm, sub, seq, rope, kv_only, d_pool, d_attn, d_kv, d_sg):
    (xp_ref, x_ref, xn_ref, mod_ref, g1_ref, wf_ref, wpool_ref, pscale_ref, sgg_ref, wsp_ref, bsp_ref) = refs[:11]
    n_in = 11
    if rope:
        cos_q_ref, sin_q_ref, cos_k_ref, sin_k_ref = refs[n_in:n_in + 4]
        n_in += 4
    if kv_only:
        k_ref, vt_ref, wqkv_t = refs[n_in:]
    else:
        ypool_ref, qt_ref, k_ref, vt_ref, ysg_ref, wqkv_t = refs[n_in:]

    @pl.when(pl.program_id(0) == 0)
    def _():
        wqkv_t[...] = wf_ref[:, d_pool:d_pool + d_attn + 2 * d_kv].T

    n_parts = tm // sub
    n_ext = sub + 2 * HALO
    tiles_per_seq = seq // tm
    t_in_seq = lax.rem(pl.program_id(0), tiles_per_seq)
    shift = mod_ref[0, 0:1, :]
    gain_scale = g1_ref[...] * (1.0 + mod_ref[0, 1:2, :])
    off_u = d_pool + d_attn + 2 * d_kv
    halo_row = lax.broadcasted_iota(jnp.int32, (HALO, LANES), 0)
    edge = SUBLANES
    edge_row = lax.broadcasted_iota(jnp.int32, (edge, LANES), 0)
    quarter = HEAD_DIM // 4
    n_q, n_kv = d_attn // HEAD_DIM, d_kv // HEAD_DIM

    def normed(x):
        return _modnorm(x, gain_scale, shift).astype(BF16)

    def project(p):
        lo = p * sub
        h_bf = normed(x_ref[lo:lo + sub, :])
        if kv_only:
            return lax.dot_general(wqkv_t[d_attn:, :], h_bf, NT_DIMS, preferred_element_type=F32), None, None
        x_prev = xp_ref[...] if p == 0 else x_ref[lo - HALO:lo, :]
        x_next = xn_ref[...] if p == n_parts - 1 else x_ref[lo + sub:lo + sub + HALO, :]
        h_ext = jnp.concatenate([normed(x_prev), h_bf, normed(x_next)], axis=0)
        zp = jnp.dot(h_ext, wf_ref[:, :d_pool], preferred_element_type=F32)
        zr = jnp.dot(h_bf, wf_ref[:, off_u:off_u + 2 * d_sg], preferred_element_type=F32)
        zt = lax.dot_general(wqkv_t[...], h_bf, NT_DIMS, preferred_element_type=F32)
        return zt, zp, zr

    def shifted(a, s):
        return pltpu.roll(a, s % n_ext, 0)

    def inv_count(pos, half):
        return 1.0 / (jnp.minimum(pos + half, seq) - jnp.maximum(pos - half, 0)).astype(F32)

    def rotate_t(blk, cos_ref, sin_ref, tok):
        swapped = jnp.concatenate([blk[quarter:2 * quarter], blk[:quarter],
                                   blk[3 * quarter:], blk[2 * quarter:3 * quarter]], axis=0)
        return blk * cos_ref[:, tok] + swapped * sin_ref[:, tok]

    def finish(p, zt, zp, zr):
        tok = slice(p * sub, (p + 1) * sub)
        start = t_in_seq * tm + p * sub

        kv = zt if kv_only else zt[d_attn:]
        k_heads = [kv[hk * HEAD_DIM:(hk + 1) * HEAD_DIM] for hk in range(n_kv)]
        k_heads = [rotate_t(blk, cos_k_ref, sin_k_ref, tok) if rope else blk * Q_SCALE for blk in k_heads]
        k_ref[tok, :] = jnp.concatenate(k_heads, axis=0).T.astype(BF16)
        vt_ref[:, tok] = kv[d_kv:].astype(BF16)
        if kv_only:
            return
        for hq in range(n_q):
            blk = zt[hq * HEAD_DIM:(hq + 1) * HEAD_DIM]
            blk = rotate_t(blk, cos_q_ref, sin_q_ref, tok) if rope else blk
            qt_ref[hq * HEAD_DIM:(hq + 1) * HEAD_DIM, tok] = blk.astype(BF16)

        keep_head = halo_row >= jnp.where(t_in_seq == 0, HALO, 0) if p == 0 else None
        keep_tail = halo_row < jnp.where(t_in_seq == tiles_per_seq - 1, 0, HALO) if p == n_parts - 1 else None
        pos_head = edge_row + start
        pos_tail = edge_row + (start + sub - edge)
        pooled = []
        for g, w in enumerate(POOL_WINDOWS):
            half = w // 2
            cols = slice(g * LANES, (g + 1) * LANES)
            head, tail = zp[:HALO, cols], zp[HALO + sub:, cols]
            e = jnp.concatenate([head if keep_head is None else jnp.where(keep_head, head, 0.0),
                                 zp[HALO:HALO + sub, cols],
                                 tail if keep_tail is None else jnp.where(keep_tail, tail, 0.0)], axis=0)
            trail = e
            s = 1
            while s < half:
                trail = trail + shifted(trail, s)
                s *= 2
            wsum = shifted(trail, -(half - 1)) + shifted(trail, 1) if half > 1 else trail + shifted(trail, 1)
            ws = wsum[HALO:HALO + sub]
            own = e[HALO:HALO + sub]
            pooled.append(jnp.concatenate([
                ws[:edge] * inv_count(pos_head, half) - own[:edge],
                ws[edge:sub - edge] * (1.0 / w) - own[edge:sub - edge],
                ws[sub - edge:] * inv_count(pos_tail, half) - own[sub - edge:]], axis=0).astype(BF16))
        for pair in range(len(POOL_WINDOWS) // 2):
            cols = slice(2 * pair * LANES, 2 * (pair + 1) * LANES)
            mixed = jnp.dot(jnp.concatenate(pooled[2 * pair:2 * pair + 2], axis=1), wpool_ref[pair],
                            preferred_element_type=F32)
            ypool_ref[tok, cols] = (mixed * pscale_ref[:, cols]).astype(BF16)

        gu = _gelu_tanh(zr[:, :d_sg])
        gs = _gelu_tanh(zr[:, d_sg:])
        ms = jnp.mean(gs * gs, axis=-1, keepdims=True)
        vn = ((gs * lax.rsqrt(ms + NORM_EPS)) * sgg_ref[...]).astype(BF16)
        chunks = [slice(c * CHUNK, (c + 1) * CHUNK) for c in range(sub // CHUNK)]
        for g in range(N_SG_GROUPS):
            cols = slice(g * LANES, (g + 1) * LANES)
            mixed = jnp.dot(wsp_ref[g], jnp.concatenate([vn[r, cols] for r in chunks], axis=1),
                            preferred_element_type=F32)
            for r in chunks:
                out_rows = slice(p * sub + r.start, p * sub + r.stop)
                ysg_ref[out_rows, cols] = (gu[r, cols] * (mixed[:, r] + bsp_ref[:, cols])).astype(BF16)

    staged = project(0)
    for p in range(n_parts):
        ahead = project(p + 1) if p + 1 < n_parts else None
        finish(p, *staged)
        staged = ahead


def _inproj(x2, mod_l, mod_row_fn, g1, wf, wpool, pscale, sgg, wsp, bsp, rope_tabs, *, tm, sub, seq, dims,
            kv_only=False):
    t, d = x2.shape
    d_pool, d_attn, d_kv, d_sg = dims
    n_tiles = t // tm
    assert max(POOL_WINDOWS) // 2 <= SUBLANES <= HALO and tm % sub == 0 and sub % CHUNK == 0
    hb = tm // HALO
    n_hblk = t // HALO
    tiles_per_seq = seq // tm
    rope = rope_tabs is not None
    in_specs = [
        pl.BlockSpec((HALO, d), lambda i: (jnp.maximum(i * hb - 1, 0), 0)),
        pl.BlockSpec((tm, d), lambda i: (i, 0)),
        pl.BlockSpec((HALO, d), lambda i: (jnp.minimum((i + 1) * hb, n_hblk - 1), 0)),
        pl.BlockSpec((1, N_MOD, d), lambda i: (mod_row_fn(i), 0, 0)),
        _const_spec(g1.shape), _const_spec(wf.shape), _const_spec(wpool.shape),
        _const_spec(pscale.shape), _const_spec(sgg.shape), _const_spec(wsp.shape), _const_spec(bsp.shape),
    ]
    args = [x2, x2, x2, mod_l, g1, wf, wpool, pscale, sgg, wsp, bsp]
    if rope:
        in_specs += [pl.BlockSpec((HEAD_DIM, tm), lambda i: (0, lax.rem(i, tiles_per_seq)))] * len(rope_tabs)
        args += list(rope_tabs)
    row = lambda i: (i, 0)
    col = lambda i: (0, i)
    outs = [((t, d_pool), (tm, d_pool), row), ((d_attn, t), (d_attn, tm), col), ((t, d_kv), (tm, d_kv), row),
            ((d_kv, t), (d_kv, tm), col), ((t, d_sg), (tm, d_sg), row)]
    if kv_only:
        outs = outs[2:4]
    return pl.pallas_call(
        functools.partial(_inproj_kernel, tm=tm, sub=sub, seq=seq, rope=rope, kv_only=kv_only, d_pool=d_pool,
                          d_attn=d_attn, d_kv=d_kv, d_sg=d_sg),
        grid=(n_tiles,),
        in_specs=in_specs,
        out_specs=[pl.BlockSpec(blk, imap) for _, blk, imap in outs],
        out_shape=[jax.ShapeDtypeStruct(shape, BF16) for shape, _, _ in outs],
        scratch_shapes=[pltpu.VMEM((d_attn + 2 * d_kv, d), BF16)],
        compiler_params=_params(1),
        name=("inproj_rope" if rope else "inproj_ctx") + ("_kv" if kv_only else ""),
    )(*args)


def _logits(q_rhs, k_cat, masks, sink_row):
    s = jnp.dot(k_cat, q_rhs, preferred_element_type=F32)
    if masks:
        pieces, at = [], 0
        for rs, mk in masks:
            if rs.start > at:
                pieces.append(s[at:rs.start])
            pieces.append(jnp.where(mk, s[rs], MASK_VALUE))
            at = rs.stop
        if at < s.shape[0]:
            pieces.append(s[at:])
        s = jnp.concatenate(pieces, axis=0)
    return s, jnp.maximum(jnp.max(s, axis=0, keepdims=True), sink_row)


def _probabilities(s, m, sink_row):
    p = jnp.exp2(s - m)
    denom = jnp.sum(p, axis=0, keepdims=True) + jnp.exp2(sink_row - m)
    return p.astype(BF16), denom


def _weighted_values(p, denom, vt_cat):
    return jnp.dot(vt_cat, p, preferred_element_type=F32) / denom


def _q_rhs(qt_ref, h, cols):
    blocks = [qt_ref[(GQA_GROUP * h + j) * HEAD_DIM:(GQA_GROUP * h + j + 1) * HEAD_DIM, cols]
              for j in range(GQA_GROUP)]
    q = jnp.concatenate(blocks, axis=1)
    z = jnp.zeros_like(q)
    return jnp.concatenate([q, z] if h == 0 else [z, q], axis=0)


def _store_heads(o_ref, h, rows, out):
    for j in range(0, GQA_GROUP, 2):
        pair = jnp.concatenate([out[:, j * ATTN_BLOCK:(j + 1) * ATTN_BLOCK],
                                out[:, (j + 1) * ATTN_BLOCK:(j + 2) * ATTN_BLOCK]], axis=0)
        lo = (GQA_GROUP * h + j) * HEAD_DIM
        o_ref[rows, lo:lo + 2 * HEAD_DIM] = pair.T.astype(BF16)


def _win_attn_kernel(*refs, qt, n_steps, casts):
    n_in = 10
    qt_ref, kp_ref, ko_ref, kn_ref, vp_ref, vo_ref, vn_ref, kc_ref, vc_ref, sink_ref = refs[:n_in]
    cast_in = refs[n_in:n_in + len(casts)]
    o_ref = refs[n_in + len(casts)]
    _run_casts(cast_in, refs[n_in + len(casts) + 1:], casts)
    step = pl.program_id(1)
    nsub = qt // ATTN_BLOCK
    key = lax.broadcasted_iota(jnp.int32, (ATTN_BLOCK, QCOLS), 0)
    qry = lax.rem(lax.broadcasted_iota(jnp.int32, (ATTN_BLOCK, QCOLS), 1), ATTN_BLOCK)
    tri_prev = key >= qry
    tri_next = key <= qry
    edge_prev = key >= qry + jnp.where(step > 0, 0, 2 * ATTN_BLOCK)
    edge_next = key <= qry - jnp.where(step < n_steps - 1, 0, 2 * ATTN_BLOCK)
    prev_rows = slice(0, ATTN_BLOCK)
    next_rows = slice(2 * ATTN_BLOCK, 3 * ATTN_BLOCK)
    sink_rows = [sink_ref[h] * LOG2_E for h in range(2)]

    def logits_stage(h, n):
        r = slice(n * ATTN_BLOCK, (n + 1) * ATTN_BLOCK)
        if n > 0:
            k_prev, m_prev = ko_ref[(n - 1) * ATTN_BLOCK:n * ATTN_BLOCK, :], tri_prev
        else:
            k_prev, m_prev = kp_ref[...], edge_prev
        if n < nsub - 1:
            k_next, m_next = ko_ref[(n + 1) * ATTN_BLOCK:(n + 2) * ATTN_BLOCK, :], tri_next
        else:
            k_next, m_next = kn_ref[...], edge_next
        k_cat = jnp.concatenate([k_prev, ko_ref[r, :], k_next, kc_ref[...]], axis=0)
        return _logits(_q_rhs(qt_ref, h, r), k_cat, [(prev_rows, m_prev), (next_rows, m_next)], sink_rows[h])

    def output_stage(h, n, p, denom):
        feat = slice(h * HEAD_DIM, (h + 1) * HEAD_DIM)
        r = slice(n * ATTN_BLOCK, (n + 1) * ATTN_BLOCK)
        v_prev = vo_ref[feat, (n - 1) * ATTN_BLOCK:n * ATTN_BLOCK] if n > 0 else vp_ref[feat, :]
        v_next = vo_ref[feat, (n + 1) * ATTN_BLOCK:(n + 2) * ATTN_BLOCK] if n < nsub - 1 else vn_ref[feat, :]
        vt_cat = jnp.concatenate([v_prev, vo_ref[feat, r], v_next, vc_ref[feat, :]], axis=1)
        _store_heads(o_ref, h, r, _weighted_values(p, denom, vt_cat))

    work = [(h, n) for h in range(2) for n in range(nsub)]
    logits, probs = {}, {}
    for i in range(len(work) + 2):
        if i < len(work):
            logits[i] = logits_stage(*work[i])
        if 0 <= i - 1 < len(work):
            probs[i - 1] = _probabilities(*logits.pop(i - 1), sink_rows[work[i - 1][0]])
        if 0 <= i - 2 < len(work):
            output_stage(*work[i - 2], *probs.pop(i - 2))


def _window_attention(q_t, k, v_t, kc, vc_t, sink_tab, casts, *, n_batch, seq, ctx_len, qt):
    dq, t = q_t.shape
    dkv = k.shape[1]
    n_steps = seq // qt
    sub = qt // ATTN_BLOCK
    blocks_per_seq = seq // ATTN_BLOCK
    own = lambda b, j: b * n_steps + j
    prev = lambda b, j: b * blocks_per_seq + jnp.maximum(j * sub - 1, 0)
    nxt = lambda b, j: b * blocks_per_seq + jnp.minimum((j + 1) * sub, blocks_per_seq - 1)
    rows = lambda f: (lambda b, j: (f(b, j), 0))
    cols = lambda f: (lambda b, j: (0, f(b, j)))
    cast_in, cast_out, cast_shapes, splits = _cast_plumbing(casts, n_batch * n_steps, own)
    return pl.pallas_call(
        functools.partial(_win_attn_kernel, qt=qt, n_steps=n_steps, casts=splits),
        grid=(n_batch, n_steps),
        in_specs=[
            pl.BlockSpec((dq, qt), cols(own)),
            pl.BlockSpec((ATTN_BLOCK, dkv), rows(prev)), pl.BlockSpec((qt, dkv), rows(own)),
            pl.BlockSpec((ATTN_BLOCK, dkv), rows(nxt)),
            pl.BlockSpec((dkv, ATTN_BLOCK), cols(prev)), pl.BlockSpec((dkv, qt), cols(own)),
            pl.BlockSpec((dkv, ATTN_BLOCK), cols(nxt)),
            pl.BlockSpec((ctx_len, dkv), lambda b, j: (b, 0)), pl.BlockSpec((dkv, ctx_len), lambda b, j: (0, b)),
            pl.BlockSpec(sink_tab.shape, lambda b, j: (0, 0, 0)),
        ] + cast_in,
        out_specs=[pl.BlockSpec((qt, dq), rows(own))] + cast_out,
        out_shape=[jax.ShapeDtypeStruct((t, dq), BF16)] + cast_shapes,
        compiler_params=_params(2),
        name="window_attention",
    )(q_t, k, k, k, v_t, v_t, v_t, kc, vc_t, sink_tab, *[w for w, _, _ in casts])


def _ctx_attn_kernel(qt_ref, k_ref, vt_ref, sink_ref, o_ref, *, ctx_len):
    sink_rows = [sink_ref[h] * LOG2_E for h in range(2)]
    blocks = [slice(n * ATTN_BLOCK, (n + 1) * ATTN_BLOCK) for n in range(ctx_len // ATTN_BLOCK)]
    work = [(h, r) for h in range(2) for r in blocks]
    logits, probs = {}, {}
    for i in range(len(work) + 2):
        if i < len(work):
            h, r = work[i]
            logits[i] = _logits(_q_rhs(qt_ref, h, r), k_ref[...], [], sink_rows[h])
        if 0 <= i - 1 < len(work):
            probs[i - 1] = _probabilities(*logits.pop(i - 1), sink_rows[work[i - 1][0]])
        if 0 <= i - 2 < len(work):
            h, r = work[i - 2]
            out = _weighted_values(*probs.pop(i - 2), vt_ref[h * HEAD_DIM:(h + 1) * HEAD_DIM, :])
            _store_heads(o_ref, h, r, out)


def _context_attention(q_t, k, v_t, sink_tab, *, n_batch, ctx_len):
    dq, t = q_t.shape
    dkv = k.shape[1]
    return pl.pallas_call(
        functools.partial(_ctx_attn_kernel, ctx_len=ctx_len),
        grid=(n_batch,),
        in_specs=[pl.BlockSpec((dq, ctx_len), lambda b: (0, b)), pl.BlockSpec((ctx_len, dkv), lambda b: (b, 0)),
                  pl.BlockSpec((dkv, ctx_len), lambda b: (0, b)), pl.BlockSpec(sink_tab.shape, lambda b: (0, 0, 0))],
        out_specs=pl.BlockSpec((ctx_len, dq), lambda b: (b, 0)),
        out_shape=jax.ShapeDtypeStruct((t, dq), BF16),
        compiler_params=_params(1),
        name="context_attention",
    )(q_t, k, v_t, sink_tab)


def _row_parts(tm, sub):
    return [slice(r, r + sub) for r in range(0, tm, sub)]


def _merge_kernel(*refs, sub, casts):
    n_in = 11
    x_ref, mod_ref, g1_ref, yp_ref, ya_ref, ys_ref, wg_ref, wbp_ref, wba_ref, wbs_ref, wo_ref = refs[:n_in]
    o_ref = refs[n_in + len(casts)]
    _run_casts(refs[n_in:n_in + len(casts)], refs[n_in + len(casts) + 1:], casts)
    tm, d = x_ref.shape
    shift, gate = mod_ref[0, 0:1, :], mod_ref[0, 2:3, :]
    gain_scale = g1_ref[...] * (1.0 + mod_ref[0, 1:2, :])
    for r in _row_parts(tm, sub):
        x = x_ref[r, :]
        hx = _modnorm(x, gain_scale, shift).astype(BF16)
        y = None
        for b, (y_ref, w_ref) in enumerate(((yp_ref, wbp_ref), (ya_ref, wba_ref), (ys_ref, wbs_ref))):
            gate_b = jax.nn.sigmoid(jnp.dot(hx, wg_ref[:, b * d:(b + 1) * d], preferred_element_type=F32))
            part = gate_b * jnp.dot(y_ref[r, :], w_ref[...], preferred_element_type=F32)
            y = part if y is None else y + part
        o = jnp.dot(y.astype(BF16), wo_ref[...], preferred_element_type=F32)
        o_ref[r, :] = x + gate * o


def _merge(x2, mod_l, mod_row_fn, g1, yp, ya, ys, wg, wbp, wba, wbs, wo, casts, *, tm, sub):
    t, d = x2.shape
    row = lambda i: (i, 0)
    cast_in, cast_out, cast_shapes, splits = _cast_plumbing(casts, t // tm, lambda i: i)
    return pl.pallas_call(
        functools.partial(_merge_kernel, sub=sub, casts=splits),
        grid=(t // tm,),
        in_specs=[
            pl.BlockSpec((tm, d), row),
            pl.BlockSpec((1, N_MOD, d), lambda i: (mod_row_fn(i), 0, 0)),
            _const_spec(g1.shape),
            pl.BlockSpec((tm, yp.shape[1]), row), pl.BlockSpec((tm, ya.shape[1]), row),
            pl.BlockSpec((tm, ys.shape[1]), row),
            _const_spec(wg.shape), _const_spec(wbp.shape), _const_spec(wba.shape), _const_spec(wbs.shape),
            _const_spec(wo.shape),
        ] + cast_in,
        out_specs=[pl.BlockSpec((tm, d), row)] + cast_out,
        out_shape=[jax.ShapeDtypeStruct((t, d), F32)] + cast_shapes,
        compiler_params=_params(1),
        name="merge",
    )(x2, mod_l, g1, yp, ya, ys, wg, wbp, wba, wbs, wo, *[w for w, _, _ in casts])


def _ffn_chunks(d_ff):
    n_tiles = d_ff // MXU_DIM
    first = (n_tiles + 1) // 2 * MXU_DIM
    return ((0, first), (first, d_ff))


def _ffn_kernel(*refs, d_ff, final, sub, casts):
    n_in = 6 if final else 5
    x_ref, mod_ref, g2_ref, wi_ref, wo_ref = refs[:5]
    fg_ref = refs[5] if final else None
    o_ref = refs[n_in + len(casts)]
    _run_casts(refs[n_in:n_in + len(casts)], refs[n_in + len(casts) + 1:], casts)
    shift, gate = mod_ref[0, 3:4, :], mod_ref[0, 5:6, :]
    gain_scale = g2_ref[...] * (1.0 + mod_ref[0, 4:5, :])
    for r in _row_parts(x_ref.shape[0], sub):
        x = x_ref[r, :]
        h = _modnorm(x, gain_scale, shift).astype(BF16)
        acc = None
        for lo, hi in _ffn_chunks(d_ff):
            a = jnp.dot(h, wi_ref[:, lo:hi], preferred_element_type=F32)
            b = jnp.dot(h, wi_ref[:, d_ff + lo:d_ff + hi], preferred_element_type=F32)
            act = ((a * jax.nn.sigmoid(a)) * b).astype(BF16)
            part = jnp.dot(act, wo_ref[lo:hi, :], preferred_element_type=F32)
            acc = part if acc is None else acc + part
        y = x + gate * acc
        if final:
            ms = jnp.mean(y * y, axis=-1, keepdims=True)
            y = (y * lax.rsqrt(ms + NORM_EPS)) * fg_ref[...]
        o_ref[r, :] = y


def _ffn(x2, mod_l, mod_row_fn, g2, wi, wo, final_gain, casts, *, tm, sub):
    t, d = x2.shape
    d_ff = wo.shape[0]
    final = final_gain is not None
    row = lambda i: (i, 0)
    in_specs = [
        pl.BlockSpec((tm, d), row),
        pl.BlockSpec((1, N_MOD, d), lambda i: (mod_row_fn(i), 0, 0)),
        _const_spec(g2.shape), _const_spec(wi.shape), _const_spec(wo.shape),
    ]
    args = [x2, mod_l, g2, wi, wo]
    if final:
        in_specs.append(_const_spec(final_gain.shape))
        args.append(final_gain)
    cast_in, cast_out, cast_shapes, splits = _cast_plumbing(casts, t // tm, lambda i: i)
    return pl.pallas_call(
        functools.partial(_ffn_kernel, d_ff=d_ff, final=final, sub=sub, casts=splits),
        grid=(t // tm,),
        in_specs=in_specs + cast_in,
        out_specs=[pl.BlockSpec((tm, d), row)] + cast_out,
        out_shape=[jax.ShapeDtypeStruct((t, d), F32)] + cast_shapes,
        compiler_params=_params(1),
        name="ffn_final" if final else "ffn",
    )(*args, *[w for w, _, _ in casts])


def _rope_tables(seq):
    rows = seq // GRID_W
    freqs = HEAD_DIM // 4
    row = jnp.repeat(jnp.arange(rows), GRID_W).astype(F32)
    col = jnp.tile(jnp.arange(GRID_W), rows).astype(F32)
    inv_freq = ROPE_THETA ** (-jnp.arange(freqs, dtype=F32) / freqs)
    ang_r = row[:, None] * inv_freq[None, :]
    ang_c = col[:, None] * inv_freq[None, :]
    cos = jnp.concatenate([jnp.cos(ang_r), jnp.cos(ang_r), jnp.cos(ang_c), jnp.cos(ang_c)], axis=-1)
    sin = jnp.concatenate([-jnp.sin(ang_r), jnp.sin(ang_r), -jnp.sin(ang_c), jnp.sin(ang_c)], axis=-1)
    return cos.T, sin.T, cos.T * Q_SCALE, sin.T * Q_SCALE


def _pick_tile(seq, target):
    tm = min(seq, target)
    assert seq % tm == 0 and tm % CHUNK == 0
    return tm


def kernel(x, c, ctx, c_ctx, w_mod, b_mod, norm1_gain, norm2_gain, w_in, w_pool, pool_scale, attn_sink, sg_v_gain,
           w_spatial, b_spatial, w_br_pool, w_br_attn, w_br_sg, w_out, w_ffn_in, w_ffn_out, final_gain):
    n_batch, seq, d = x.shape
    ctx_len = ctx.shape[1]
    depth = w_mod.shape[0]
    d_pool = w_br_pool.shape[1]
    d_attn = w_br_attn.shape[1]
    d_sg = w_br_sg.shape[1]
    n_q_heads = attn_sink.shape[1]
    d_kv = (n_q_heads // GQA_GROUP) * HEAD_DIM
    off_k = d_pool + d_attn
    off_v = off_k + d_kv
    off_u = off_v + d_kv
    off_gate = off_u + 2 * d_sg
    assert n_q_heads == 2 * GQA_GROUP and d_kv == LANES
    assert d_pool == len(POOL_WINDOWS) * LANES and d_sg == N_SG_GROUPS * LANES
    assert w_in.shape[2] == off_gate + N_BRANCHES * d and n_batch + 1 <= MOD_ROWS
    assert seq % GRID_W == 0 and w_spatial.shape[-1] == CHUNK
    dims = (d_pool, d_attn, d_kv, d_sg)

    tm_c = _pick_tile(ctx_len, 512)
    qt = _pick_tile(seq, 512)
    big_x = _pick_tile(seq, 1024)
    big_c = _pick_tile(n_batch * ctx_len, 1024)
    sub_x, sub_c = min(big_x, 512), min(big_c, 512)

    all_cols = lambda w: ((0, w.shape[2]),)
    w_in_cast = lambda layer: [(w_in, layer, ((0, off_gate), (off_gate, w_in.shape[2])))]

    cc = jnp.concatenate([c, c_ctx[None, :], jnp.zeros((MOD_ROWS - n_batch - 1, d), F32)], axis=0)
    mod, wf, wg = _modulation(cc, w_mod, b_mod, w_in_cast(0))
    mod = mod.reshape(depth, MOD_ROWS, N_MOD, d)

    rope_tabs = _rope_tables(seq)
    big_tiles_per_seq = seq // big_x
    mod_row_big = lambda i: i // big_tiles_per_seq
    mod_row_c = lambda i: n_batch

    xs = x.reshape(n_batch * seq, d)
    cs = ctx.reshape(n_batch * ctx_len, d)
    row2 = lambda a: a.reshape(1, -1)

    for l in range(depth):
        last = l == depth - 1
        mod_l = mod[l]
        g1, g2 = row2(norm1_gain[l]), row2(norm2_gain[l])
        zblk = jnp.zeros((LANES, LANES), F32)
        wpool = jnp.stack([jnp.block([[w_pool[l, 2 * p], zblk], [zblk, w_pool[l, 2 * p + 1]]])
                           for p in range(len(POOL_WINDOWS) // 2)]).astype(BF16)
        wsp = w_spatial[l].astype(BF16)
        pscale, sgg = row2(pool_scale[l]), row2(sg_v_gain[l])
        bsp = jnp.repeat(b_spatial[l].T, LANES, axis=1)
        sink_tab = jnp.repeat(attn_sink[l].reshape(2, GQA_GROUP), ATTN_BLOCK, axis=1)[:, None, :]

        common = (g1, wf, wpool, pscale, sgg, wsp, bsp)
        if last:
            kc, vc_t = _inproj(cs, mod_l, mod_row_c, *common, None, tm=tm_c, sub=tm_c, seq=ctx_len, dims=dims,
                               kv_only=True)
        else:
            ypc, qc_t, kc, vc_t, ysc = _inproj(cs, mod_l, mod_row_c, *common, None, tm=tm_c, sub=tm_c,
                                                 seq=ctx_len, dims=dims)
        ypx, qx_t, kx, vx_t, ysx = _inproj(xs, mod_l, mod_row_big, *common, rope_tabs, tm=big_x,
                                             sub=sub_x, seq=seq, dims=dims)
        merge_casts = [(w, l, all_cols(w)) for w in (w_br_pool, w_br_attn, w_br_sg, w_out)]
        ax, wbp, wba, wbs, wo = _window_attention(qx_t, kx, vx_t, kc, vc_t, sink_tab, merge_casts,
                                                  n_batch=n_batch, seq=seq, ctx_len=ctx_len, qt=qt)
        ffn_casts = [(w, l, all_cols(w)) for w in (w_ffn_in, w_ffn_out)]
        xs, wfi, wfo = _merge(xs, mod_l, mod_row_big, g1, ypx, ax, ysx, wg, wbp, wba, wbs, wo, ffn_casts,
                              tm=big_x, sub=sub_x)
        if last:
            xs, = _ffn(xs, mod_l, mod_row_big, g2, wfi, wfo, row2(final_gain), [], tm=big_x, sub=sub_x)
        else:
            ac = _context_attention(qc_t, kc, vc_t, sink_tab, n_batch=n_batch, ctx_len=ctx_len)
            cs, = _merge(cs, mod_l, mod_row_c, g1, ypc, ac, ysc, wg, wbp, wba, wbs, wo, [], tm=big_c, sub=sub_c)
            cs, = _ffn(cs, mod_l, mod_row_c, g2, wfi, wfo, None, [], tm=big_c, sub=sub_c)
            xs, wf, wg = _ffn(xs, mod_l, mod_row_big, g2, wfi, wfo, None, w_in_cast(l + 1), tm=big_x, sub=sub_x)
    return xs.reshape(n_batch, seq, d)
```

```python
import functools
import math

import jax
import jax.numpy as jnp
from jax import lax
from jax.experimental import pallas as pl
from jax.experimental.pallas import tpu as pltpu

F32 = jnp.float32
BF16 = jnp.bfloat16

GRID_W = 64
NORM_EPS = 1e-6
MASK_VALUE = -1e30
N_MOD = 6
POOL_WINDOWS = (2, 4, 8, 16)
HEAD_DIM = 64
GQA_GROUP = 4
ATTN_BLOCK = 128
ROPE_THETA = 10000.0
CHUNK = 128
N_SG_GROUPS = 4
N_BRANCHES = 3
LOG2_E = math.log2(math.e)

LANES = 128
SUBLANES = 8
MXU_DIM = 256
VMEM_LIMIT_BYTES = 56 * 1024 * 1024

BF16_ROWS = 2 * SUBLANES
MOD_ROWS = BF16_ROWS
HALO = BF16_ROWS
Q_SCALE = HEAD_DIM ** -0.5 * LOG2_E
QCOLS = GQA_GROUP * ATTN_BLOCK

NT_DIMS = (((1,), (1,)), ((), ()))


def _const_spec(shape):
    nd = len(shape)
    return pl.BlockSpec(shape, lambda *_: (0,) * nd, pipeline_mode=pl.Buffered(1))


def _params(n_axes=1):
    return pltpu.CompilerParams(dimension_semantics=("arbitrary",) * n_axes,
                                vmem_limit_bytes=VMEM_LIMIT_BYTES)


def _cast_plumbing(casts, n_steps, step_of):
    in_specs, out_specs, out_shapes, splits = [], [], [], []
    for w, layer, cols in casts:
        _, r, c = w.shape
        rows = r // n_steps
        assert rows * n_steps == r and rows % BF16_ROWS == 0, (w.shape, n_steps)
        in_specs.append(pl.BlockSpec((1, rows, c), lambda *g, layer=layer: (layer, step_of(*g), 0)))
        for lo, hi in cols:
            out_specs.append(pl.BlockSpec((rows, hi - lo), lambda *g: (step_of(*g), 0)))
            out_shapes.append(jax.ShapeDtypeStruct((r, hi - lo), BF16))
        splits.append(tuple(cols))
    return in_specs, out_specs, out_shapes, tuple(splits)


def _run_casts(cast_in, cast_out, splits):
    k = 0
    for src, cols in zip(cast_in, splits):
        for lo, hi in cols:
            cast_out[k][...] = src[0, :, lo:hi].astype(BF16)
            k += 1


def _modnorm(xf, gain_scale, shift):
    ms = jnp.mean(xf * xf, axis=-1, keepdims=True)
    return (xf * lax.rsqrt(ms + NORM_EPS)) * gain_scale + shift


def _gelu_tanh(x):
    c0 = -2.0 * math.sqrt(2.0 / math.pi) * LOG2_E
    c1 = c0 * 0.044715
    return x / (1.0 + jnp.exp2(x * (c0 + c1 * (x * x))))


def _mod_kernel(*refs, casts):
    c_ref, w_ref, b_ref = refs[:3]
    o_ref = refs[3 + len(casts)]
    _run_casts(refs[3:3 + len(casts)], refs[3 + len(casts) + 1:], casts)
    c = c_ref[...]
    s = (c * jax.nn.sigmoid(c)).astype(BF16)
    o_ref[0] = jnp.dot(s, w_ref[0].astype(BF16), preferred_element_type=F32) + b_ref[0]


def _modulation(cc, w_mod, b_mod, casts):
    depth, d, n = w_mod.shape
    tn = 768
    n_col = n // tn
    assert n_col * tn == n
    cast_in, cast_out, cast_shapes, splits = _cast_plumbing(casts, depth * n_col, lambda l, j: l * n_col + j)
    return pl.pallas_call(
        functools.partial(_mod_kernel, casts=splits),
        grid=(depth, n_col),
        in_specs=[
            pl.BlockSpec((MOD_ROWS, d), lambda l, j: (0, 0)),
            pl.BlockSpec((1, d, tn), lambda l, j: (l, 0, j)),
            pl.BlockSpec((1, 1, tn), lambda l, j: (l, 0, j)),
        ] + cast_in,
        out_specs=[pl.BlockSpec((1, MOD_ROWS, tn), lambda l, j: (l, 0, j))] + cast_out,
        out_shape=[jax.ShapeDtypeStruct((depth, MOD_ROWS, n), F32)] + cast_shapes,
        compiler_params=_params(2),
        name="modulation",
    )(cc, w_mod, b_mod.reshape(depth, 1, n), *[w for w, _, _ in casts])


def _inproj_kernel(*refs, tm, sub, seq, rope, kv_only, d_pool, d_attn, d_kv, d_sg):
    (xp_ref, x_ref, xn_ref, mod_ref, g1_ref, wf_ref, wpool_ref, pscale_ref, sgg_ref, wsp_ref, bsp_ref) = refs[:11]
    n_in = 11
    if rope:
        cos_q_ref, sin_q_ref, cos_k_ref, sin_k_ref = refs[n_in:n_in + 4]
        n_in += 4
    if kv_only:
        k_ref, vt_ref, wqkv_t = refs[n_in:]
    else:
        ypool_ref, qt_ref, k_ref, vt_ref, ysg_ref, wqkv_t = refs[n_in:]

    @pl.when(pl.program_id(0) == 0)
    def _():
        wqkv_t[...] = wf_ref[:, d_pool:d_pool + d_attn + 2 * d_kv].T

    n_parts = tm // sub
    n_ext = sub + 2 * HALO
    tiles_per_seq = seq // tm
    t_in_seq = lax.rem(pl.program_id(0), tiles_per_seq)
    shift = mod_ref[0, 0:1, :]
    gain_scale = g1_ref[...] * (1.0 + mod_ref[0, 1:2, :])
    off_u = d_pool + d_attn + 2 * d_kv
    halo_row = lax.broadcasted_iota(jnp.int32, (HALO, LANES), 0)
    edge = SUBLANES
    edge_row = lax.broadcasted_iota(jnp.int32, (edge, LANES), 0)
    quarter = HEAD_DIM // 4
    n_q, n_kv = d_attn // HEAD_DIM, d_kv // HEAD_DIM

    def normed(x):
        return _modnorm(x, gain_scale, shift).astype(BF16)

    def project(p):
        lo = p * sub
        h_bf = normed(x_ref[lo:lo + sub, :])
        if kv_only:
            return lax.dot_general(wqkv_t[d_attn:, :], h_bf, NT_DIMS, preferred_element_type=F32), None, None
        x_prev = xp_ref[...] if p == 0 else x_ref[lo - HALO:lo, :]
        x_next = xn_ref[...] if p == n_parts - 1 else x_ref[lo + sub:lo + sub + HALO, :]
        h_ext = jnp.concatenate([normed(x_prev), h_bf, normed(x_next)], axis=0)
        zp = jnp.dot(h_ext, wf_ref[:, :d_pool], preferred_element_type=F32)
        zr = jnp.dot(h_bf, wf_ref[:, off_u:off_u + 2 * d_sg], preferred_element_type=F32)
        zt = lax.dot_general(wqkv_t[...], h_bf, NT_DIMS, preferred_element_type=F32)
        return zt, zp, zr

    def shifted(a, s):
        return pltpu.roll(a, s % n_ext, 0)

    def inv_count(pos, half):
        return 1.0 / (jnp.minimum(pos + half, seq) - jnp.maximum(pos - half, 0)).astype(F32)

    def rotate_t(blk, cos_ref, sin_ref, tok):
        swapped = jnp.concatenate([blk[quarter:2 * quarter], blk[:quarter],
                                   blk[3 * quarter:], blk[2 * quarter:3 * quarter]], axis=0)
        return blk * cos_ref[:, tok] + swapped * sin_ref[:, tok]

    def finish(p, zt, zp, zr):
        tok = slice(p * sub, (p + 1) * sub)
        start = t_in_seq * tm + p * sub

        kv = zt if kv_only else zt[d_attn:]
        k_heads = [kv[hk * HEAD_DIM:(hk + 1) * HEAD_DIM] for hk in range(n_kv)]
        k_heads = [rotate_t(blk, cos_k_ref, sin_k_ref, tok) if rope else blk * Q_SCALE for blk in k_heads]
        k_ref[tok, :] = jnp.concatenate(k_heads, axis=0).T.astype(BF16)
        vt_ref[:, tok] = kv[d_kv:].astype(BF16)
        if kv_only:
            return
        for hq in range(n_q):
            blk = zt[hq * HEAD_DIM:(hq + 1) * HEAD_DIM]
            blk = rotate_t(blk, cos_q_ref, sin_q_ref, tok) if rope else blk
            qt_ref[hq * HEAD_DIM:(hq + 1) * HEAD_DIM, tok] = blk.astype(BF16)

        keep_head = halo_row >= jnp.where(t_in_seq == 0, HALO, 0) if p == 0 else None
        keep_tail = halo_row < jnp.where(t_in_seq == tiles_per_seq - 1, 0, HALO) if p == n_parts - 1 else None
        pos_head = edge_row + start
        pos_tail = edge_row + (start + sub - edge)
        pooled = []
        for g, w in enumerate(POOL_WINDOWS):
            half = w // 2
            cols = slice(g * LANES, (g + 1) * LANES)
            head, tail = zp[:HALO, cols], zp[HALO + sub:, cols]
            e = jnp.concatenate([head if keep_head is None else jnp.where(keep_head, head, 0.0),
                                 zp[HALO:HALO + sub, cols],
                                 tail if keep_tail is None else jnp.where(keep_tail, tail, 0.0)], axis=0)
            trail = e
            s = 1
            while s < half:
                trail = trail + shifted(trail, s)
                s *= 2
            wsum = shifted(trail, -(half - 1)) + shifted(trail, 1) if half > 1 else trail + shifted(trail, 1)
            ws = wsum[HALO:HALO + sub]
            own = e[HALO:HALO + sub]
            pooled.append(jnp.concatenate([
                ws[:edge] * inv_count(pos_head, half) - own[:edge],
                ws[edge:sub - edge] * (1.0 / w) - own[edge:sub - edge],
                ws[sub - edge:] * inv_count(pos_tail, half) - own[sub - edge:]], axis=0).astype(BF16))
        for pair in range(len(POOL_WINDOWS) // 2):
            cols = slice(2 * pair * LANES, 2 * (pair + 1) * LANES)
            mixed = jnp.dot(jnp.concatenate(pooled[2 * pair:2 * pair + 2], axis=1), wpool_ref[pair],
                            preferred_element_type=F32)
            ypool_ref[tok, cols] = (mixed * pscale_ref[:, cols]).astype(BF16)

        gu = _gelu_tanh(zr[:, :d_sg])
        gs = _gelu_tanh(zr[:, d_sg:])
        ms = jnp.mean(gs * gs, axis=-1, keepdims=True)
        vn = ((gs * lax.rsqrt(ms + NORM_EPS)) * sgg_ref[...]).astype(BF16)
        chunks = [slice(c * CHUNK, (c + 1) * CHUNK) for c in range(sub // CHUNK)]
        for g in range(N_SG_GROUPS):
            cols = slice(g * LANES, (g + 1) * LANES)
            mixed = jnp.dot(wsp_ref[g], jnp.concatenate([vn[r, cols] for r in chunks], axis=1),
                            preferred_element_type=F32)
            for r in chunks:
                out_rows = slice(p * sub + r.start, p * sub + r.stop)
                ysg_ref[out_rows, cols] = (gu[r, cols] * (mixed[:, r] + bsp_ref[:, cols])).astype(BF16)

    staged = project(0)
    for p in range(n_parts):
        ahead = project(p + 1) if p + 1 < n_parts else None
        finish(p, *staged)
        staged = ahead


def _inproj(x2, mod_l, mod_row_fn, g1, wf, wpool, pscale, sgg, wsp, bsp, rope_tabs, *, tm, sub, seq, dims,
            kv_only=False):
    t, d = x2.shape
    d_pool, d_attn, d_kv, d_sg = dims
    n_tiles = t // tm
    assert max(POOL_WINDOWS) // 2 <= SUBLANES <= HALO and tm % sub == 0 and sub % CHUNK == 0
    hb = tm // HALO
    n_hblk = t // HALO
    tiles_per_seq = seq // tm
    rope = rope_tabs is not None
    in_specs = [
        pl.BlockSpec((HALO, d), lambda i: (jnp.maximum(i * hb - 1, 0), 0)),
        pl.BlockSpec((tm, d), lambda i: (i, 0)),
        pl.BlockSpec((HALO, d), lambda i: (jnp.minimum((i + 1) * hb, n_hblk - 1), 0)),
        pl.BlockSpec((1, N_MOD, d), lambda i: (mod_row_fn(i), 0, 0)),
        _const_spec(g1.shape), _const_spec(wf.shape), _const_spec(wpool.shape),
        _const_spec(pscale.shape), _const_spec(sgg.shape), _const_spec(wsp.shape), _const_spec(bsp.shape),
    ]
    args = [x2, x2, x2, mod_l, g1, wf, wpool, pscale, sgg, wsp, bsp]
    if rope:
        in_specs += [pl.BlockSpec((HEAD_DIM, tm), lambda i: (0, lax.rem(i, tiles_per_seq)))] * len(rope_tabs)
        args += list(rope_tabs)
    row = lambda i: (i, 0)
    col = lambda i: (0, i)
    outs = [((t, d_pool), (tm, d_pool), row), ((d_attn, t), (d_attn, tm), col), ((t, d_kv), (tm, d_kv), row),
            ((d_kv, t), (d_kv, tm), col), ((t, d_sg), (tm, d_sg), row)]
    if kv_only:
        outs = outs[2:4]
    return pl.pallas_call(
        functools.partial(_inproj_kernel, tm=tm, sub=sub, seq=seq, rope=rope, kv_only=kv_only, d_pool=d_pool,
                          d_attn=d_attn, d_kv=d_kv, d_sg=d_sg),
        grid=(n_tiles,),
        in_specs=in_specs,
        out_specs=[pl.BlockSpec(blk, imap) for _, blk, imap in outs],
        out_shape=[jax.ShapeDtypeStruct(shape, BF16) for shape, _, _ in outs],
        scratch_shapes=[pltpu.VMEM((d_attn + 2 * d_kv, d), BF16)],
        compiler_params=_params(1),
        name=("inproj_rope" if rope else "inproj_ctx") + ("_kv" if kv_only else ""),
    )(*args)


def _logits(q_rhs, k_cat, masks, sink_row):
    s = jnp.dot(k_cat, q_rhs, preferred_element_type=F32)
    if masks:
        pieces, at = [], 0
        for rs, mk in masks:
            if rs.start > at:
                pieces.append(s[at:rs.start])
            pieces.append(s[rs] + mk)
            at = rs.stop
        if at < s.shape[0]:
            pieces.append(s[at:])
        s = jnp.concatenate(pieces, axis=0)
    return s, jnp.maximum(jnp.max(s, axis=0, keepdims=True), sink_row)


def _probabilities(s, m, sink_row):
    p = jnp.exp2(s - m)
    denom = jnp.sum(p, axis=0, keepdims=True) + jnp.exp2(sink_row - m)
    return p.astype(BF16), denom


def _weighted_values(p, denom, vt_cat):
    return jnp.dot(vt_cat, p, preferred_element_type=F32) / denom


def _q_rhs(qt_ref, h, cols):
    blocks = [qt_ref[(GQA_GROUP * h + j) * HEAD_DIM:(GQA_GROUP * h + j + 1) * HEAD_DIM, cols]
              for j in range(GQA_GROUP)]
    q = jnp.concatenate(blocks, axis=1)
    z = jnp.zeros_like(q)
    return jnp.concatenate([q, z] if h == 0 else [z, q], axis=0)


def _store_heads(o_ref, h, rows, out):
    for j in range(0, GQA_GROUP, 2):
        pair = jnp.concatenate([out[:, j * ATTN_BLOCK:(j + 1) * ATTN_BLOCK],
                                out[:, (j + 1) * ATTN_BLOCK:(j + 2) * ATTN_BLOCK]], axis=0)
        lo = (GQA_GROUP * h + j) * HEAD_DIM
        o_ref[rows, lo:lo + 2 * HEAD_DIM] = pair.T.astype(BF16)


def _win_attn_kernel(*refs, qt, n_steps, casts):
    n_in = 10
    qt_ref, kp_ref, ko_ref, kn_ref, vp_ref, vo_ref, vn_ref, kc_ref, vc_ref, sink_ref = refs[:n_in]
    cast_in = refs[n_in:n_in + len(casts)]
    o_ref = refs[n_in + len(casts)]
    _run_casts(cast_in, refs[n_in + len(casts) + 1:], casts)
    step = pl.program_id(1)
    nsub = qt // ATTN_BLOCK
    key = lax.broadcasted_iota(jnp.int32, (ATTN_BLOCK, QCOLS), 0)
    qry = lax.rem(lax.broadcasted_iota(jnp.int32, (ATTN_BLOCK, QCOLS), 1), ATTN_BLOCK)
    bias = lambda valid: jnp.where(valid, 0.0, MASK_VALUE)
    tri_prev = bias(key >= qry)
    tri_next = bias(key <= qry)
    edge_prev = bias(key >= qry + jnp.where(step > 0, 0, 2 * ATTN_BLOCK))
    edge_next = bias(key <= qry - jnp.where(step < n_steps - 1, 0, 2 * ATTN_BLOCK))
    prev_rows = slice(0, ATTN_BLOCK)
    next_rows = slice(2 * ATTN_BLOCK, 3 * ATTN_BLOCK)
    sink_rows = [sink_ref[h] * LOG2_E for h in range(2)]

    def logits_stage(h, n):
        r = slice(n * ATTN_BLOCK, (n + 1) * ATTN_BLOCK)
        if n > 0:
            k_prev, m_prev = ko_ref[(n - 1) * ATTN_BLOCK:n * ATTN_BLOCK, :], tri_prev
        else:
            k_prev, m_prev = kp_ref[...], edge_prev
        if n < nsub - 1:
            k_next, m_next = ko_ref[(n + 1) * ATTN_BLOCK:(n + 2) * ATTN_BLOCK, :], tri_next
        else:
            k_next, m_next = kn_ref[...], edge_next
        k_cat = jnp.concatenate([k_prev, ko_ref[r, :], k_next, kc_ref[...]], axis=0)
        return _logits(_q_rhs(qt_ref, h, r), k_cat, [(prev_rows, m_prev), (next_rows, m_next)], sink_rows[h])

    def output_stage(h, n, p, denom):
        feat = slice(h * HEAD_DIM, (h + 1) * HEAD_DIM)
        r = slice(n * ATTN_BLOCK, (n + 1) * ATTN_BLOCK)
        v_prev = vo_ref[feat, (n - 1) * ATTN_BLOCK:n * ATTN_BLOCK] if n > 0 else vp_ref[feat, :]
        v_next = vo_ref[feat, (n + 1) * ATTN_BLOCK:(n + 2) * ATTN_BLOCK] if n < nsub - 1 else vn_ref[feat, :]
        vt_cat = jnp.concatenate([v_prev, vo_ref[feat, r], v_next, vc_ref[feat, :]], axis=1)
        _store_heads(o_ref, h, r, _weighted_values(p, denom, vt_cat))

    work = [(h, n) for h in range(2) for n in range(nsub)]
    logits, probs = {}, {}
    for i in range(len(work) + 2):
        if i < len(work):
            logits[i] = logits_stage(*work[i])
        if 0 <= i - 1 < len(work):
            probs[i - 1] = _probabilities(*logits.pop(i - 1), sink_rows[work[i - 1][0]])
        if 0 <= i - 2 < len(work):
            output_stage(*work[i - 2], *probs.pop(i - 2))


def _window_attention(q_t, k, v_t, kc, vc_t, sink_tab, casts, *, n_batch, seq, ctx_len, qt):
    dq, t = q_t.shape
    dkv = k.shape[1]
    n_steps = seq // qt
    sub = qt // ATTN_BLOCK
    blocks_per_seq = seq // ATTN_BLOCK
    own = lambda b, j: b * n_steps + j
    prev = lambda b, j: b * blocks_per_seq + jnp.maximum(j * sub - 1, 0)
    nxt = lambda b, j: b * blocks_per_seq + jnp.minimum((j + 1) * sub, blocks_per_seq - 1)
    rows = lambda f: (lambda b, j: (f(b, j), 0))
    cols = lambda f: (lambda b, j: (0, f(b, j)))
    cast_in, cast_out, cast_shapes, splits = _cast_plumbing(casts, n_batch * n_steps, own)
    return pl.pallas_call(
        functools.partial(_win_attn_kernel, qt=qt, n_steps=n_steps, casts=splits),
        grid=(n_batch, n_steps),
        in_specs=[
            pl.BlockSpec((dq, qt), cols(own)),
            pl.BlockSpec((ATTN_BLOCK, dkv), rows(prev)), pl.BlockSpec((qt, dkv), rows(own)),
            pl.BlockSpec((ATTN_BLOCK, dkv), rows(nxt)),
            pl.BlockSpec((dkv, ATTN_BLOCK), cols(prev)), pl.BlockSpec((dkv, qt), cols(own)),
            pl.BlockSpec((dkv, ATTN_BLOCK), cols(nxt)),
            pl.BlockSpec((ctx_len, dkv), lambda b, j: (b, 0)), pl.BlockSpec((dkv, ctx_len), lambda b, j: (0, b)),
            pl.BlockSpec(sink_tab.shape, lambda b, j: (0, 0, 0)),
        ] + cast_in,
        out_specs=[pl.BlockSpec((qt, dq), rows(own))] + cast_out,
        out_shape=[jax.ShapeDtypeStruct((t, dq), BF16)] + cast_shapes,
        compiler_params=_params(2),
        name="window_attention",
    )(q_t, k, k, k, v_t, v_t, v_t, kc, vc_t, sink_tab, *[w for w, _, _ in casts])


def _ctx_attn_kernel(qt_ref, k_ref, vt_ref, sink_ref, o_ref, *, ctx_len):
    sink_rows = [sink_ref[h] * LOG2_E for h in range(2)]
    blocks = [slice(n * ATTN_BLOCK, (n + 1) * ATTN_BLOCK) for n in range(ctx_len // ATTN_BLOCK)]
    work = [(h, r) for h in range(2) for r in blocks]
    logits, probs = {}, {}
    for i in range(len(work) + 2):
        if i < len(work):
            h, r = work[i]
            logits[i] = _logits(_q_rhs(qt_ref, h, r), k_ref[...], [], sink_rows[h])
        if 0 <= i - 1 < len(work):
            probs[i - 1] = _probabilities(*logits.pop(i - 1), sink_rows[work[i - 1][0]])
        if 0 <= i - 2 < len(work):
            h, r = work[i - 2]
            out = _weighted_values(*probs.pop(i - 2), vt_ref[h * HEAD_DIM:(h + 1) * HEAD_DIM, :])
            _store_heads(o_ref, h, r, out)


def _context_attention(q_t, k, v_t, sink_tab, *, n_batch, ctx_len):
    dq, t = q_t.shape
    dkv = k.shape[1]
    return pl.pallas_call(
        functools.partial(_ctx_attn_kernel, ctx_len=ctx_len),
        grid=(n_batch,),
        in_specs=[pl.BlockSpec((dq, ctx_len), lambda b: (0, b)), pl.BlockSpec((ctx_len, dkv), lambda b: (b, 0)),
                  pl.BlockSpec((dkv, ctx_len), lambda b: (0, b)), pl.BlockSpec(sink_tab.shape, lambda b: (0, 0, 0))],
        out_specs=pl.BlockSpec((ctx_len, dq), lambda b: (b, 0)),
        out_shape=jax.ShapeDtypeStruct((t, dq), BF16),
        compiler_params=_params(1),
        name="context_attention",
    )(q_t, k, v_t, sink_tab)


def _row_parts(tm, sub):
    return [slice(r, r + sub) for r in range(0, tm, sub)]


def _merge_kernel(*refs, sub, casts):
    n_in = 11
    x_ref, mod_ref, g1_ref, yp_ref, ya_ref, ys_ref, wg_ref, wbp_ref, wba_ref, wbs_ref, wo_ref = refs[:n_in]
    o_ref = refs[n_in + len(casts)]
    _run_casts(refs[n_in:n_in + len(casts)], refs[n_in + len(casts) + 1:], casts)
    tm, d = x_ref.shape
    shift, gate = mod_ref[0, 0:1, :], mod_ref[0, 2:3, :]
    gain_scale = g1_ref[...] * (1.0 + mod_ref[0, 1:2, :])
    for r in _row_parts(tm, sub):
        x = x_ref[r, :]
        hx = _modnorm(x, gain_scale, shift).astype(BF16)
        y = None
        for b, (y_ref, w_ref) in enumerate(((yp_ref, wbp_ref), (ya_ref, wba_ref), (ys_ref, wbs_ref))):
            gate_b = jax.nn.sigmoid(jnp.dot(hx, wg_ref[:, b * d:(b + 1) * d], preferred_element_type=F32))
            part = gate_b * jnp.dot(y_ref[r, :], w_ref[...], preferred_element_type=F32)
            y = part if y is None else y + part
        o = jnp.dot(y.astype(BF16), wo_ref[...], preferred_element_type=F32)
        o_ref[r, :] = x + gate * o


def _merge(x2, mod_l, mod_row_fn, g1, yp, ya, ys, wg, wbp, wba, wbs, wo, casts, *, tm, sub):
    t, d = x2.shape
    row = lambda i: (i, 0)
    cast_in, cast_out, cast_shapes, splits = _cast_plumbing(casts, t // tm, lambda i: i)
    return pl.pallas_call(
        functools.partial(_merge_kernel, sub=sub, casts=splits),
        grid=(t // tm,),
        in_specs=[
            pl.BlockSpec((tm, d), row),
            pl.BlockSpec((1, N_MOD, d), lambda i: (mod_row_fn(i), 0, 0)),
            _const_spec(g1.shape),
            pl.BlockSpec((tm, yp.shape[1]), row), pl.BlockSpec((tm, ya.shape[1]), row),
            pl.BlockSpec((tm, ys.shape[1]), row),
            _const_spec(wg.shape), _const_spec(wbp.shape), _const_spec(wba.shape), _const_spec(wbs.shape),
            _const_spec(wo.shape),
        ] + cast_in,
        out_specs=[pl.BlockSpec((tm, d), row)] + cast_out,
        out_shape=[jax.ShapeDtypeStruct((t, d), F32)] + cast_shapes,
        compiler_params=_params(1),
        name="merge",
    )(x2, mod_l, g1, yp, ya, ys, wg, wbp, wba, wbs, wo, *[w for w, _, _ in casts])


def _ffn_chunks(d_ff):
    n_tiles = d_ff // MXU_DIM
    first = (n_tiles + 1) // 2 * MXU_DIM
    return ((0, first), (first, d_ff))


def _ffn_kernel(*refs, d_ff, final, sub, casts):
    n_in = 6 if final else 5
    x_ref, mod_ref, g2_ref, wi_ref, wo_ref = refs[:5]
    fg_ref = refs[5] if final else None
    o_ref = refs[n_in + len(casts)]
    _run_casts(refs[n_in:n_in + len(casts)], refs[n_in + len(casts) + 1:], casts)
    shift, gate = mod_ref[0, 3:4, :], mod_ref[0, 5:6, :]
    gain_scale = g2_ref[...] * (1.0 + mod_ref[0, 4:5, :])
    for r in _row_parts(x_ref.shape[0], sub):
        x = x_ref[r, :]
        h = _modnorm(x, gain_scale, shift).astype(BF16)
        acc = None
        for lo, hi in _ffn_chunks(d_ff):
            a = jnp.dot(h, wi_ref[:, lo:hi], preferred_element_type=F32)
            b = jnp.dot(h, wi_ref[:, d_ff + lo:d_ff + hi], preferred_element_type=F32)
            act = ((a * jax.nn.sigmoid(a)) * b).astype(BF16)
            part = jnp.dot(act, wo_ref[lo:hi, :], preferred_element_type=F32)
            acc = part if acc is None else acc + part
        y = x + gate * acc
        if final:
            ms = jnp.mean(y * y, axis=-1, keepdims=True)
            y = (y * lax.rsqrt(ms + NORM_EPS)) * fg_ref[...]
        o_ref[r, :] = y


def _ffn(x2, mod_l, mod_row_fn, g2, wi, wo, final_gain, casts, *, tm, sub):
    t, d = x2.shape
    d_ff = wo.shape[0]
    final = final_gain is not None
    row = lambda i: (i, 0)
    in_specs = [
        pl.BlockSpec((tm, d), row),
        pl.BlockSpec((1, N_MOD, d), lambda i: (mod_row_fn(i), 0, 0)),
        _const_spec(g2.shape), _const_spec(wi.shape), _const_spec(wo.shape),
    ]
    args = [x2, mod_l, g2, wi, wo]
    if final:
        in_specs.append(_const_spec(final_gain.shape))
        args.append(final_gain)
    cast_in, cast_out, cast_shapes, splits = _cast_plumbing(casts, t // tm, lambda i: i)
    return pl.pallas_call(
        functools.partial(_ffn_kernel, d_ff=d_ff, final=final, sub=sub, casts=splits),
        grid=(t // tm,),
        in_specs=in_specs + cast_in,
        out_specs=[pl.BlockSpec((tm, d), row)] + cast_out,
        out_shape=[jax.ShapeDtypeStruct((t, d), F32)] + cast_shapes,
        compiler_params=_params(1),
        name="ffn_final" if final else "ffn",
    )(*args, *[w for w, _, _ in casts])


def _rope_tables(seq):
    rows = seq // GRID_W
    freqs = HEAD_DIM // 4
    row = jnp.repeat(jnp.arange(rows), GRID_W).astype(F32)
    col = jnp.tile(jnp.arange(GRID_W), rows).astype(F32)
    inv_freq = ROPE_THETA ** (-jnp.arange(freqs, dtype=F32) / freqs)
    ang_r = row[:, None] * inv_freq[None, :]
    ang_c = col[:, None] * inv_freq[None, :]
    cos = jnp.concatenate([jnp.cos(ang_r), jnp.cos(ang_r), jnp.cos(ang_c), jnp.cos(ang_c)], axis=-1)
    sin = jnp.concatenate([-jnp.sin(ang_r), jnp.sin(ang_r), -jnp.sin(ang_c), jnp.sin(ang_c)], axis=-1)
    return cos.T, sin.T, cos.T * Q_SCALE, sin.T * Q_SCALE


def _pick_tile(seq, target):
    tm = min(seq, target)
    assert seq % tm == 0 and tm % CHUNK == 0
    return tm


def kernel(x, c, ctx, c_ctx, w_mod, b_mod, norm1_gain, norm2_gain, w_in, w_pool, pool_scale, attn_sink, sg_v_gain,
           w_spatial, b_spatial, w_br_pool, w_br_attn, w_br_sg, w_out, w_ffn_in, w_ffn_out, final_gain):
    n_batch, seq, d = x.shape
    ctx_len = ctx.shape[1]
    depth = w_mod.shape[0]
    d_pool = w_br_pool.shape[1]
    d_attn = w_br_attn.shape[1]
    d_sg = w_br_sg.shape[1]
    n_q_heads = attn_sink.shape[1]
    d_kv = (n_q_heads // GQA_GROUP) * HEAD_DIM
    off_k = d_pool + d_attn
    off_v = off_k + d_kv
    off_u = off_v + d_kv
    off_gate = off_u + 2 * d_sg
    assert n_q_heads == 2 * GQA_GROUP and d_kv == LANES
    assert d_pool == len(POOL_WINDOWS) * LANES and d_sg == N_SG_GROUPS * LANES
    assert w_in.shape[2] == off_gate + N_BRANCHES * d and n_batch + 1 <= MOD_ROWS
    assert seq % GRID_W == 0 and w_spatial.shape[-1] == CHUNK
    dims = (d_pool, d_attn, d_kv, d_sg)

    tm_c = _pick_tile(ctx_len, 512)
    qt = _pick_tile(seq, 1024)
    big_x = _pick_tile(seq, 1024)
    big_c = _pick_tile(n_batch * ctx_len, 1024)
    sub_x, sub_c = min(big_x, 512), min(big_c, 512)

    all_cols = lambda w: ((0, w.shape[2]),)
    w_in_cast = lambda layer: [(w_in, layer, ((0, off_gate), (off_gate, w_in.shape[2])))]

    cc = jnp.concatenate([c, c_ctx[None, :], jnp.zeros((MOD_ROWS - n_batch - 1, d), F32)], axis=0)
    mod, wf, wg = _modulation(cc, w_mod, b_mod, w_in_cast(0))
    mod = mod.reshape(depth, MOD_ROWS, N_MOD, d)

    rope_tabs = _rope_tables(seq)
    big_tiles_per_seq = seq // big_x
    mod_row_big = lambda i: i // big_tiles_per_seq
    mod_row_c = lambda i: n_batch

    xs = x.reshape(n_batch * seq, d)
    cs = ctx.reshape(n_batch * ctx_len, d)
    row2 = lambda a: a.reshape(1, -1)

    for l in range(depth):
        last = l == depth - 1
        mod_l = mod[l]
        g1, g2 = row2(norm1_gain[l]), row2(norm2_gain[l])
        zblk = jnp.zeros((LANES, LANES), F32)
        wpool = jnp.stack([jnp.block([[w_pool[l, 2 * p], zblk], [zblk, w_pool[l, 2 * p + 1]]])
                           for p in range(len(POOL_WINDOWS) // 2)]).astype(BF16)
        wsp = w_spatial[l].astype(BF16)
        pscale, sgg = row2(pool_scale[l]), row2(sg_v_gain[l])
        bsp = jnp.repeat(b_spatial[l].T, LANES, axis=1)
        sink_tab = jnp.repeat(attn_sink[l].reshape(2, GQA_GROUP), ATTN_BLOCK, axis=1)[:, None, :]

        common = (g1, wf, wpool, pscale, sgg, wsp, bsp)
        if last:
            kc, vc_t = _inproj(cs, mod_l, mod_row_c, *common, None, tm=tm_c, sub=tm_c, seq=ctx_len, dims=dims,
                               kv_only=True)
        else:
            ypc, qc_t, kc, vc_t, ysc = _inproj(cs, mod_l, mod_row_c, *common, None, tm=tm_c, sub=tm_c,
                                                 seq=ctx_len, dims=dims)
        ypx, qx_t, kx, vx_t, ysx = _inproj(xs, mod_l, mod_row_big, *common, rope_tabs, tm=big_x,
                                             sub=sub_x, seq=seq, dims=dims)
        merge_casts = [(w, l, all_cols(w)) for w in (w_br_pool, w_br_attn, w_br_sg, w_out)]
        ax, wbp, wba, wbs, wo = _window_attention(qx_t, kx, vx_t, kc, vc_t, sink_tab, merge_casts,
                                                  n_batch=n_batch, seq=seq, ctx_len=ctx_len, qt=qt)
        ffn_casts = [(w, l, all_cols(w)) for w in (w_ffn_in, w_ffn_out)]
        xs, wfi, wfo = _merge(xs, mod_l, mod_row_big, g1, ypx, ax, ysx, wg, wbp, wba, wbs, wo, ffn_casts,
                              tm=big_x, sub=sub_x)
        if last:
            xs, = _ffn(xs, mod_l, mod_row_big, g2, wfi, wfo, row2(final_gain), [], tm=big_x, sub=sub_x)
        else:
            ac = _context_attention(qc_t, kc, vc_t, sink_tab, n_batch=n_batch, ctx_len=ctx_len)
            cs, = _merge(cs, mod_l, mod_row_c, g1, ypc, ac, ysc, wg, wbp, wba, wbs, wo, [], tm=big_c, sub=sub_c)
            cs, = _ffn(cs, mod_l, mod_row_c, g2, wfi, wfo, None, [], tm=big_c, sub=sub_c)
            xs, wf, wg = _ffn(xs, mod_l, mod_row_big, g2, wfi, wfo, None, w_in_cast(l + 1), tm=big_x, sub=sub_x)
    return xs.reshape(n_batch, seq, d)
```

```python
import functools
import math

import jax
import jax.numpy as jnp
from jax import lax
from jax.experimental import pallas as pl
from jax.experimental.pallas import tpu as pltpu

F32 = jnp.float32
BF16 = jnp.bfloat16

GRID_W = 64
NORM_EPS = 1e-6
MASK_VALUE = -1e30
N_MOD = 6
POOL_WINDOWS = (2, 4, 8, 16)
HEAD_DIM = 64
GQA_GROUP = 4
ATTN_BLOCK = 128
ROPE_THETA = 10000.0
CHUNK = 128
N_SG_GROUPS = 4
N_BRANCHES = 3
LOG2_E = math.log2(math.e)

LANES = 128
SUBLANES = 8
MXU_DIM = 256
VMEM_LIMIT_BYTES = 56 * 1024 * 1024

BF16_ROWS = 2 * SUBLANES
MOD_ROWS = BF16_ROWS
HALO = BF16_ROWS
Q_SCALE = HEAD_DIM ** -0.5 * LOG2_E
WIN_HEADS_PER_ITEM = GQA_GROUP

NT_DIMS = (((1,), (1,)), ((), ()))


def _const_spec(shape):
    nd = len(shape)
    return pl.BlockSpec(shape, lambda *_: (0,) * nd, pipeline_mode=pl.Buffered(1))


def _params(n_axes=1):
    return pltpu.CompilerParams(dimension_semantics=("arbitrary",) * n_axes,
                                vmem_limit_bytes=VMEM_LIMIT_BYTES)


def _cast_plumbing(casts, n_steps, step_of):
    in_specs, out_specs, out_shapes, splits = [], [], [], []
    for w, layer, cols in casts:
        _, r, c = w.shape
        rows = r // n_steps
        assert rows * n_steps == r and rows % BF16_ROWS == 0, (w.shape, n_steps)
        in_specs.append(pl.BlockSpec((1, rows, c), lambda *g, layer=layer: (layer, step_of(*g), 0)))
        for lo, hi in cols:
            out_specs.append(pl.BlockSpec((rows, hi - lo), lambda *g: (step_of(*g), 0)))
            out_shapes.append(jax.ShapeDtypeStruct((r, hi - lo), BF16))
        splits.append(tuple(cols))
    return in_specs, out_specs, out_shapes, tuple(splits)


def _run_casts(cast_in, cast_out, splits):
    k = 0
    for src, cols in zip(cast_in, splits):
        for lo, hi in cols:
            cast_out[k][...] = src[0, :, lo:hi].astype(BF16)
            k += 1


def _modnorm(xf, gain_scale, shift):
    ms = jnp.mean(xf * xf, axis=-1, keepdims=True)
    return (xf * lax.rsqrt(ms + NORM_EPS)) * gain_scale + shift


def _gelu_tanh(x):
    c0 = -2.0 * math.sqrt(2.0 / math.pi) * LOG2_E
    c1 = c0 * 0.044715
    return x / (1.0 + jnp.exp2(x * (c0 + c1 * (x * x))))


def _mod_kernel(*refs, casts):
    c_ref, w_ref, b_ref = refs[:3]
    o_ref = refs[3 + len(casts)]
    _run_casts(refs[3:3 + len(casts)], refs[3 + len(casts) + 1:], casts)
    c = c_ref[...]
    s = (c * jax.nn.sigmoid(c)).astype(BF16)
    o_ref[0] = jnp.dot(s, w_ref[0].astype(BF16), preferred_element_type=F32) + b_ref[0]


def _modulation(cc, w_mod, b_mod, casts):
    depth, d, n = w_mod.shape
    tn = 768
    n_col = n // tn
    assert n_col * tn == n
    cast_in, cast_out, cast_shapes, splits = _cast_plumbing(casts, depth * n_col, lambda l, j: l * n_col + j)
    return pl.pallas_call(
        functools.partial(_mod_kernel, casts=splits),
        grid=(depth, n_col),
        in_specs=[
            pl.BlockSpec((MOD_ROWS, d), lambda l, j: (0, 0)),
            pl.BlockSpec((1, d, tn), lambda l, j: (l, 0, j)),
            pl.BlockSpec((1, 1, tn), lambda l, j: (l, 0, j)),
        ] + cast_in,
        out_specs=[pl.BlockSpec((1, MOD_ROWS, tn), lambda l, j: (l, 0, j))] + cast_out,
        out_shape=[jax.ShapeDtypeStruct((depth, MOD_ROWS, n), F32)] + cast_shapes,
        compiler_params=_params(2),
        name="modulation",
    )(cc, w_mod, b_mod.reshape(depth, 1, n), *[w for w, _, _ in casts])


def _inproj_kernel(*refs, tm, sub, seq, rope, kv_only, d_pool, d_attn, d_kv, d_sg):
    (xp_ref, x_ref, xn_ref, mod_ref, g1_ref, wf_ref, wpool_ref, pscale_ref, sgg_ref, wsp_ref, bsp_ref) = refs[:11]
    n_in = 11
    if rope:
        cos_q_ref, sin_q_ref, cos_k_ref, sin_k_ref = refs[n_in:n_in + 4]
        n_in += 4
    if kv_only:
        k_ref, vt_ref, wqkv_t = refs[n_in:]
    else:
        ypool_ref, qt_ref, k_ref, vt_ref, ysg_ref, wqkv_t = refs[n_in:]

    @pl.when(pl.program_id(0) == 0)
    def _():
        wqkv_t[...] = wf_ref[:, d_pool:d_pool + d_attn + 2 * d_kv].T

    n_parts = tm // sub
    n_ext = sub + 2 * HALO
    tiles_per_seq = seq // tm
    t_in_seq = lax.rem(pl.program_id(0), tiles_per_seq)
    shift = mod_ref[0, 0:1, :]
    gain_scale = g1_ref[...] * (1.0 + mod_ref[0, 1:2, :])
    off_u = d_pool + d_attn + 2 * d_kv
    halo_row = lax.broadcasted_iota(jnp.int32, (HALO, LANES), 0)
    edge = SUBLANES
    edge_row = lax.broadcasted_iota(jnp.int32, (edge, LANES), 0)
    quarter = HEAD_DIM // 4
    n_q, n_kv = d_attn // HEAD_DIM, d_kv // HEAD_DIM

    def normed(x):
        return _modnorm(x, gain_scale, shift).astype(BF16)

    def project(p):
        lo = p * sub
        h_bf = normed(x_ref[lo:lo + sub, :])
        if kv_only:
            return lax.dot_general(wqkv_t[d_attn:, :], h_bf, NT_DIMS, preferred_element_type=F32), None, None
        x_prev = xp_ref[...] if p == 0 else x_ref[lo - HALO:lo, :]
        x_next = xn_ref[...] if p == n_parts - 1 else x_ref[lo + sub:lo + sub + HALO, :]
        h_ext = jnp.concatenate([normed(x_prev), h_bf, normed(x_next)], axis=0)
        zp = jnp.dot(h_ext, wf_ref[:, :d_pool], preferred_element_type=F32)
        zr = jnp.dot(h_bf, wf_ref[:, off_u:off_u + 2 * d_sg], preferred_element_type=F32)
        zt = lax.dot_general(wqkv_t[...], h_bf, NT_DIMS, preferred_element_type=F32)
        return zt, zp, zr

    def shifted(a, s):
        return pltpu.roll(a, s % n_ext, 0)

    def inv_count(pos, half):
        return 1.0 / (jnp.minimum(pos + half, seq) - jnp.maximum(pos - half, 0)).astype(F32)

    def rotate_t(blk, cos_ref, sin_ref, tok):
        swapped = jnp.concatenate([blk[quarter:2 * quarter], blk[:quarter],
                                   blk[3 * quarter:], blk[2 * quarter:3 * quarter]], axis=0)
        return blk * cos_ref[:, tok] + swapped * sin_ref[:, tok]

    def finish(p, zt, zp, zr):
        tok = slice(p * sub, (p + 1) * sub)
        start = t_in_seq * tm + p * sub

        kv = zt if kv_only else zt[d_attn:]
        k_heads = [kv[hk * HEAD_DIM:(hk + 1) * HEAD_DIM] for hk in range(n_kv)]
        k_heads = [rotate_t(blk, cos_k_ref, sin_k_ref, tok) if rope else blk * Q_SCALE for blk in k_heads]
        k_ref[tok, :] = jnp.concatenate(k_heads, axis=0).T.astype(BF16)
        vt_ref[:, tok] = kv[d_kv:].astype(BF16)
        if kv_only:
            return
        for hq in range(n_q):
            blk = zt[hq * HEAD_DIM:(hq + 1) * HEAD_DIM]
            blk = rotate_t(blk, cos_q_ref, sin_q_ref, tok) if rope else blk
            qt_ref[hq * HEAD_DIM:(hq + 1) * HEAD_DIM, tok] = blk.astype(BF16)

        keep_head = halo_row >= jnp.where(t_in_seq == 0, HALO, 0) if p == 0 else None
        keep_tail = halo_row < jnp.where(t_in_seq == tiles_per_seq - 1, 0, HALO) if p == n_parts - 1 else None
        pos_head = edge_row + start
        pos_tail = edge_row + (start + sub - edge)
        pooled = []
        for g, w in enumerate(POOL_WINDOWS):
            half = w // 2
            cols = slice(g * LANES, (g + 1) * LANES)
            head, tail = zp[:HALO, cols], zp[HALO + sub:, cols]
            e = jnp.concatenate([head if keep_head is None else jnp.where(keep_head, head, 0.0),
                                 zp[HALO:HALO + sub, cols],
                                 tail if keep_tail is None else jnp.where(keep_tail, tail, 0.0)], axis=0)
            trail = e
            s = 1
            while s < half:
                trail = trail + shifted(trail, s)
                s *= 2
            wsum = shifted(trail, -(half - 1)) + shifted(trail, 1) if half > 1 else trail + shifted(trail, 1)
            ws = wsum[HALO:HALO + sub]
            own = e[HALO:HALO + sub]
            pooled.append(jnp.concatenate([
                ws[:edge] * inv_count(pos_head, half) - own[:edge],
                ws[edge:sub - edge] * (1.0 / w) - own[edge:sub - edge],
                ws[sub - edge:] * inv_count(pos_tail, half) - own[sub - edge:]], axis=0).astype(BF16))
        for pair in range(len(POOL_WINDOWS) // 2):
            cols = slice(2 * pair * LANES, 2 * (pair + 1) * LANES)
            mixed = jnp.dot(jnp.concatenate(pooled[2 * pair:2 * pair + 2], axis=1), wpool_ref[pair],
                            preferred_element_type=F32)
            ypool_ref[tok, cols] = (mixed * pscale_ref[:, cols]).astype(BF16)

        gu = _gelu_tanh(zr[:, :d_sg])
        gs = _gelu_tanh(zr[:, d_sg:])
        ms = jnp.mean(gs * gs, axis=-1, keepdims=True)
        vn = ((gs * lax.rsqrt(ms + NORM_EPS)) * sgg_ref[...]).astype(BF16)
        chunks = [slice(c * CHUNK, (c + 1) * CHUNK) for c in range(sub // CHUNK)]
        for g in range(N_SG_GROUPS):
            cols = slice(g * LANES, (g + 1) * LANES)
            mixed = jnp.dot(wsp_ref[g], jnp.concatenate([vn[r, cols] for r in chunks], axis=1),
                            preferred_element_type=F32)
            for r in chunks:
                out_rows = slice(p * sub + r.start, p * sub + r.stop)
                ysg_ref[out_rows, cols] = (gu[r, cols] * (mixed[:, r] + bsp_ref[:, cols])).astype(BF16)

    staged = project(0)
    for p in range(n_parts):
        ahead = project(p + 1) if p + 1 < n_parts else None
        finish(p, *staged)
        staged = ahead


def _inproj(x2, mod_l, mod_row_fn, g1, wf, wpool, pscale, sgg, wsp, bsp, rope_tabs, *, tm, sub, seq, dims,
            kv_only=False):
    t, d = x2.shape
    d_pool, d_attn, d_kv, d_sg = dims
    n_tiles = t // tm
    assert max(POOL_WINDOWS) // 2 <= SUBLANES <= HALO and tm % sub == 0 and sub % CHUNK == 0
    hb = tm // HALO
    n_hblk = t // HALO
    tiles_per_seq = seq // tm
    rope = rope_tabs is not None
    in_specs = [
        pl.BlockSpec((HALO, d), lambda i: (jnp.maximum(i * hb - 1, 0), 0)),
        pl.BlockSpec((tm, d), lambda i: (i, 0)),
        pl.BlockSpec((HALO, d), lambda i: (jnp.minimum((i + 1) * hb, n_hblk - 1), 0)),
        pl.BlockSpec((1, N_MOD, d), lambda i: (mod_row_fn(i), 0, 0)),
        _const_spec(g1.shape), _const_spec(wf.shape), _const_spec(wpool.shape),
        _const_spec(pscale.shape), _const_spec(sgg.shape), _const_spec(wsp.shape), _const_spec(bsp.shape),
    ]
    args = [x2, x2, x2, mod_l, g1, wf, wpool, pscale, sgg, wsp, bsp]
    if rope:
        in_specs += [pl.BlockSpec((HEAD_DIM, tm), lambda i: (0, lax.rem(i, tiles_per_seq)))] * len(rope_tabs)
        args += list(rope_tabs)
    row = lambda i: (i, 0)
    col = lambda i: (0, i)
    outs = [((t, d_pool), (tm, d_pool), row), ((d_attn, t), (d_attn, tm), col), ((t, d_kv), (tm, d_kv), row),
            ((d_kv, t), (d_kv, tm), col), ((t, d_sg), (tm, d_sg), row)]
    if kv_only:
        outs = outs[2:4]
    return pl.pallas_call(
        functools.partial(_inproj_kernel, tm=tm, sub=sub, seq=seq, rope=rope, kv_only=kv_only, d_pool=d_pool,
                          d_attn=d_attn, d_kv=d_kv, d_sg=d_sg),
        grid=(n_tiles,),
        in_specs=in_specs,
        out_specs=[pl.BlockSpec(blk, imap) for _, blk, imap in outs],
        out_shape=[jax.ShapeDtypeStruct(shape, BF16) for shape, _, _ in outs],
        scratch_shapes=[pltpu.VMEM((d_attn + 2 * d_kv, d), BF16)],
        compiler_params=_params(1),
        name=("inproj_rope" if rope else "inproj_ctx") + ("_kv" if kv_only else ""),
    )(*args)


def _logits(q_rhs, k_cat, masks, sink_row):
    s = jnp.dot(k_cat, q_rhs, preferred_element_type=F32)
    if masks:
        pieces, at = [], 0
        for rs, mk in masks:
            if rs.start > at:
                pieces.append(s[at:rs.start])
            pieces.append(s[rs] + mk)
            at = rs.stop
        if at < s.shape[0]:
            pieces.append(s[at:])
        s = jnp.concatenate(pieces, axis=0)
    return s, jnp.maximum(jnp.max(s, axis=0, keepdims=True), sink_row)


def _probabilities(s, m, sink_row):
    p = jnp.exp2(s - m)
    denom = jnp.sum(p, axis=0, keepdims=True) + jnp.exp2(sink_row - m)
    return p.astype(BF16), denom


def _weighted_values(p, denom, vt_cat):
    return jnp.dot(vt_cat, p, preferred_element_type=F32) / denom


def _q_rhs(qt_ref, h, heads, cols):
    q = jnp.concatenate([qt_ref[hq * HEAD_DIM:(hq + 1) * HEAD_DIM, cols] for hq in heads], axis=1)
    z = jnp.zeros_like(q)
    return jnp.concatenate([q, z] if h == 0 else [z, q], axis=0)


def _store_heads(o_ref, heads, rows, out):
    for j in range(0, len(heads), 2):
        assert heads[j + 1] == heads[j] + 1
        pair = jnp.concatenate([out[:, j * ATTN_BLOCK:(j + 1) * ATTN_BLOCK],
                                out[:, (j + 1) * ATTN_BLOCK:(j + 2) * ATTN_BLOCK]], axis=0)
        lo = heads[j] * HEAD_DIM
        o_ref[rows, lo:lo + 2 * HEAD_DIM] = pair.T.astype(BF16)


def _head_groups(h, per_item):
    heads = list(range(GQA_GROUP * h, GQA_GROUP * (h + 1)))
    return [heads[i:i + per_item] for i in range(0, GQA_GROUP, per_item)]


def _win_attn_kernel(*refs, seq, casts):
    n_in = 6
    qt_ref, k_ref, vt_ref, kc_ref, vc_ref, sink_ref = refs[:n_in]
    cast_in = refs[n_in:n_in + len(casts)]
    o_ref = refs[n_in + len(casts)]
    _run_casts(cast_in, refs[n_in + len(casts) + 1:], casts)
    n_blocks = seq // ATTN_BLOCK
    qcols = WIN_HEADS_PER_ITEM * ATTN_BLOCK
    key = lax.broadcasted_iota(jnp.int32, (ATTN_BLOCK, qcols), 0)
    qry = lax.rem(lax.broadcasted_iota(jnp.int32, (ATTN_BLOCK, qcols), 1), ATTN_BLOCK)
    tri_prev = jnp.where(key >= qry, 0.0, MASK_VALUE)
    tri_next = jnp.where(key <= qry, 0.0, MASK_VALUE)
    sink_all = [sink_ref[h] * LOG2_E for h in range(2)]
    blk = lambda n: slice(n * ATTN_BLOCK, (n + 1) * ATTN_BLOCK)

    def sink_of(heads):
        h, first = divmod(heads[0], GQA_GROUP)
        return sink_all[h][:, first * ATTN_BLOCK:(first + len(heads)) * ATTN_BLOCK]

    def key_blocks(n):
        prev = [(n - 1, tri_prev)] if n > 0 else []
        nxt = [(n + 1, tri_next)] if n < n_blocks - 1 else []
        return prev + [(n, None)] + nxt

    def logits_stage(heads, n):
        h = heads[0] // GQA_GROUP
        blocks = key_blocks(n)
        k_cat = jnp.concatenate([k_ref[blk(m), :] for m, _ in blocks] + [kc_ref[...]], axis=0)
        masks = [(blk(i), bias) for i, (_, bias) in enumerate(blocks) if bias is not None]
        return _logits(_q_rhs(qt_ref, h, heads, blk(n)), k_cat, masks, sink_of(heads))

    def output_stage(heads, n, p, denom):
        h = heads[0] // GQA_GROUP
        feat = slice(h * HEAD_DIM, (h + 1) * HEAD_DIM)
        vt_cat = jnp.concatenate([vt_ref[feat, blk(m)] for m, _ in key_blocks(n)] + [vc_ref[feat, :]], axis=1)
        _store_heads(o_ref, heads, blk(n), _weighted_values(p, denom, vt_cat))

    work = [(heads, n) for h in range(2) for n in range(n_blocks) for heads in _head_groups(h, WIN_HEADS_PER_ITEM)]
    logits, probs = {}, {}
    for i in range(len(work) + 2):
        if i < len(work):
            logits[i] = logits_stage(*work[i])
        if 0 <= i - 1 < len(work):
            probs[i - 1] = _probabilities(*logits.pop(i - 1), sink_of(work[i - 1][0]))
        if 0 <= i - 2 < len(work):
            output_stage(*work[i - 2], *probs.pop(i - 2))


def _window_attention(q_t, k, v_t, kc, vc_t, sink_tab, casts, *, n_batch, seq, ctx_len):
    dq, t = q_t.shape
    dkv = k.shape[1]
    rows = lambda b: (b, 0)
    cols = lambda b: (0, b)
    cast_in, cast_out, cast_shapes, splits = _cast_plumbing(casts, n_batch, lambda b: b)
    return pl.pallas_call(
        functools.partial(_win_attn_kernel, seq=seq, casts=splits),
        grid=(n_batch,),
        in_specs=[
            pl.BlockSpec((dq, seq), cols), pl.BlockSpec((seq, dkv), rows), pl.BlockSpec((dkv, seq), cols),
            pl.BlockSpec((ctx_len, dkv), rows), pl.BlockSpec((dkv, ctx_len), cols),
            pl.BlockSpec(sink_tab.shape, lambda b: (0, 0, 0)),
        ] + cast_in,
        out_specs=[pl.BlockSpec((seq, dq), rows)] + cast_out,
        out_shape=[jax.ShapeDtypeStruct((t, dq), BF16)] + cast_shapes,
        compiler_params=_params(1),
        name="window_attention",
    )(q_t, k, v_t, kc, vc_t, sink_tab, *[w for w, _, _ in casts])


def _ctx_attn_kernel(qt_ref, k_ref, vt_ref, sink_ref, o_ref, *, ctx_len):
    sink_rows = [sink_ref[h] * LOG2_E for h in range(2)]
    blocks = [slice(n * ATTN_BLOCK, (n + 1) * ATTN_BLOCK) for n in range(ctx_len // ATTN_BLOCK)]
    work = [(h, r) for h in range(2) for r in blocks]
    logits, probs = {}, {}
    for i in range(len(work) + 2):
        if i < len(work):
            h, r = work[i]
            logits[i] = _logits(_q_rhs(qt_ref, h, _head_groups(h, GQA_GROUP)[0], r), k_ref[...], [], sink_rows[h])
        if 0 <= i - 1 < len(work):
            probs[i - 1] = _probabilities(*logits.pop(i - 1), sink_rows[work[i - 1][0]])
        if 0 <= i - 2 < len(work):
            h, r = work[i - 2]
            out = _weighted_values(*probs.pop(i - 2), vt_ref[h * HEAD_DIM:(h + 1) * HEAD_DIM, :])
            _store_heads(o_ref, _head_groups(h, GQA_GROUP)[0], r, out)


def _context_attention(q_t, k, v_t, sink_tab, *, n_batch, ctx_len):
    dq, t = q_t.shape
    dkv = k.shape[1]
    return pl.pallas_call(
        functools.partial(_ctx_attn_kernel, ctx_len=ctx_len),
        grid=(n_batch,),
        in_specs=[pl.BlockSpec((dq, ctx_len), lambda b: (0, b)), pl.BlockSpec((ctx_len, dkv), lambda b: (b, 0)),
                  pl.BlockSpec((dkv, ctx_len), lambda b: (0, b)), pl.BlockSpec(sink_tab.shape, lambda b: (0, 0, 0))],
        out_specs=pl.BlockSpec((ctx_len, dq), lambda b: (b, 0)),
        out_shape=jax.ShapeDtypeStruct((t, dq), BF16),
        compiler_params=_params(1),
        name="context_attention",
    )(q_t, k, v_t, sink_tab)


def _row_parts(tm, sub):
    return [slice(r, r + sub) for r in range(0, tm, sub)]


def _merge_kernel(*refs, sub, casts):
    n_in = 11
    x_ref, mod_ref, g1_ref, yp_ref, ya_ref, ys_ref, wg_ref, wbp_ref, wba_ref, wbs_ref, wo_ref = refs[:n_in]
    o_ref = refs[n_in + len(casts)]
    _run_casts(refs[n_in:n_in + len(casts)], refs[n_in + len(casts) + 1:], casts)
    tm, d = x_ref.shape
    shift, gate = mod_ref[0, 0:1, :], mod_ref[0, 2:3, :]
    gain_scale = g1_ref[...] * (1.0 + mod_ref[0, 1:2, :])
    for r in _row_parts(tm, sub):
        x = x_ref[r, :]
        hx = _modnorm(x, gain_scale, shift).astype(BF16)
        y = None
        for b, (y_ref, w_ref) in enumerate(((yp_ref, wbp_ref), (ya_ref, wba_ref), (ys_ref, wbs_ref))):
            gate_b = jax.nn.sigmoid(jnp.dot(hx, wg_ref[:, b * d:(b + 1) * d], preferred_element_type=F32))
            part = gate_b * jnp.dot(y_ref[r, :], w_ref[...], preferred_element_type=F32)
            y = part if y is None else y + part
        o = jnp.dot(y.astype(BF16), wo_ref[...], preferred_element_type=F32)
        o_ref[r, :] = x + gate * o


def _merge(x2, mod_l, mod_row_fn, g1, yp, ya, ys, wg, wbp, wba, wbs, wo, casts, *, tm, sub):
    t, d = x2.shape
    row = lambda i: (i, 0)
    cast_in, cast_out, cast_shapes, splits = _cast_plumbing(casts, t // tm, lambda i: i)
    return pl.pallas_call(
        functools.partial(_merge_kernel, sub=sub, casts=splits),
        grid=(t // tm,),
        in_specs=[
            pl.BlockSpec((tm, d), row),
            pl.BlockSpec((1, N_MOD, d), lambda i: (mod_row_fn(i), 0, 0)),
            _const_spec(g1.shape),
            pl.BlockSpec((tm, yp.shape[1]), row), pl.BlockSpec((tm, ya.shape[1]), row),
            pl.BlockSpec((tm, ys.shape[1]), row),
            _const_spec(wg.shape), _const_spec(wbp.shape), _const_spec(wba.shape), _const_spec(wbs.shape),
            _const_spec(wo.shape),
        ] + cast_in,
        out_specs=[pl.BlockSpec((tm, d), row)] + cast_out,
        out_shape=[jax.ShapeDtypeStruct((t, d), F32)] + cast_shapes,
        compiler_params=_params(1),
        name="merge",
    )(x2, mod_l, g1, yp, ya, ys, wg, wbp, wba, wbs, wo, *[w for w, _, _ in casts])


def _ffn_chunks(d_ff):
    n_tiles = d_ff // MXU_DIM
    first = (n_tiles + 1) // 2 * MXU_DIM
    return ((0, first), (first, d_ff))


def _ffn_kernel(*refs, d_ff, final, sub, casts):
    n_in = 6 if final else 5
    x_ref, mod_ref, g2_ref, wi_ref, wo_ref = refs[:5]
    fg_ref = refs[5] if final else None
    o_ref = refs[n_in + len(casts)]
    _run_casts(refs[n_in:n_in + len(casts)], refs[n_in + len(casts) + 1:], casts)
    shift, gate = mod_ref[0, 3:4, :], mod_ref[0, 5:6, :]
    gain_scale = g2_ref[...] * (1.0 + mod_ref[0, 4:5, :])
    for r in _row_parts(x_ref.shape[0], sub):
        x = x_ref[r, :]
        h = _modnorm(x, gain_scale, shift).astype(BF16)
        acc = None
        for lo, hi in _ffn_chunks(d_ff):
            a = jnp.dot(h, wi_ref[:, lo:hi], preferred_element_type=F32)
            b = jnp.dot(h, wi_ref[:, d_ff + lo:d_ff + hi], preferred_element_type=F32)
            act = ((a * jax.nn.sigmoid(a)) * b).astype(BF16)
            part = jnp.dot(act, wo_ref[lo:hi, :], preferred_element_type=F32)
            acc = part if acc is None else acc + part
        y = x + gate * acc
        if final:
            ms = jnp.mean(y * y, axis=-1, keepdims=True)
            y = (y * lax.rsqrt(ms + NORM_EPS)) * fg_ref[...]
        o_ref[r, :] = y


def _ffn(x2, mod_l, mod_row_fn, g2, wi, wo, final_gain, casts, *, tm, sub):
    t, d = x2.shape
    d_ff = wo.shape[0]
    final = final_gain is not None
    row = lambda i: (i, 0)
    in_specs = [
        pl.BlockSpec((tm, d), row),
        pl.BlockSpec((1, N_MOD, d), lambda i: (mod_row_fn(i), 0, 0)),
        _const_spec(g2.shape), _const_spec(wi.shape), _const_spec(wo.shape),
    ]
    args = [x2, mod_l, g2, wi, wo]
    if final:
        in_specs.append(_const_spec(final_gain.shape))
        args.append(final_gain)
    cast_in, cast_out, cast_shapes, splits = _cast_plumbing(casts, t // tm, lambda i: i)
    return pl.pallas_call(
        functools.partial(_ffn_kernel, d_ff=d_ff, final=final, sub=sub, casts=splits),
        grid=(t // tm,),
        in_specs=in_specs + cast_in,
        out_specs=[pl.BlockSpec((tm, d), row)] + cast_out,
        out_shape=[jax.ShapeDtypeStruct((t, d), F32)] + cast_shapes,
        compiler_params=_params(1),
        name="ffn_final" if final else "ffn",
    )(*args, *[w for w, _, _ in casts])


def _rope_tables(seq):
    rows = seq // GRID_W
    freqs = HEAD_DIM // 4
    row = jnp.repeat(jnp.arange(rows), GRID_W).astype(F32)
    col = jnp.tile(jnp.arange(GRID_W), rows).astype(F32)
    inv_freq = ROPE_THETA ** (-jnp.arange(freqs, dtype=F32) / freqs)
    ang_r = row[:, None] * inv_freq[None, :]
    ang_c = col[:, None] * inv_freq[None, :]
    cos = jnp.concatenate([jnp.cos(ang_r), jnp.cos(ang_r), jnp.cos(ang_c), jnp.cos(ang_c)], axis=-1)
    sin = jnp.concatenate([-jnp.sin(ang_r), jnp.sin(ang_r), -jnp.sin(ang_c), jnp.sin(ang_c)], axis=-1)
    return cos.T, sin.T, cos.T * Q_SCALE, sin.T * Q_SCALE


def _pick_tile(seq, target):
    tm = min(seq, target)
    assert seq % tm == 0 and tm % CHUNK == 0
    return tm


def kernel(x, c, ctx, c_ctx, w_mod, b_mod, norm1_gain, norm2_gain, w_in, w_pool, pool_scale, attn_sink, sg_v_gain,
           w_spatial, b_spatial, w_br_pool, w_br_attn, w_br_sg, w_out, w_ffn_in, w_ffn_out, final_gain):
    n_batch, seq, d = x.shape
    ctx_len = ctx.shape[1]
    depth = w_mod.shape[0]
    d_pool = w_br_pool.shape[1]
    d_attn = w_br_attn.shape[1]
    d_sg = w_br_sg.shape[1]
    n_q_heads = attn_sink.shape[1]
    d_kv = (n_q_heads // GQA_GROUP) * HEAD_DIM
    off_k = d_pool + d_attn
    off_v = off_k + d_kv
    off_u = off_v + d_kv
    off_gate = off_u + 2 * d_sg
    assert n_q_heads == 2 * GQA_GROUP and d_kv == LANES
    assert d_pool == len(POOL_WINDOWS) * LANES and d_sg == N_SG_GROUPS * LANES
    assert w_in.shape[2] == off_gate + N_BRANCHES * d and n_batch + 1 <= MOD_ROWS
    assert seq % GRID_W == 0 and w_spatial.shape[-1] == CHUNK
    dims = (d_pool, d_attn, d_kv, d_sg)

    tm_c = _pick_tile(ctx_len, 512)
    big_x = _pick_tile(seq, 1024)
    big_c = _pick_tile(n_batch * ctx_len, 1024)
    sub_x, sub_c = min(big_x, 512), min(big_c, 512)

    all_cols = lambda w: ((0, w.shape[2]),)
    w_in_cast = lambda layer: [(w_in, layer, ((0, off_gate), (off_gate, w_in.shape[2])))]

    cc = jnp.concatenate([c, c_ctx[None, :], jnp.zeros((MOD_ROWS - n_batch - 1, d), F32)], axis=0)
    mod, wf, wg = _modulation(cc, w_mod, b_mod, w_in_cast(0))
    mod = mod.reshape(depth, MOD_ROWS, N_MOD, d)

    rope_tabs = _rope_tables(seq)
    big_tiles_per_seq = seq // big_x
    mod_row_big = lambda i: i // big_tiles_per_seq
    mod_row_c = lambda i: n_batch

    xs = x.reshape(n_batch * seq, d)
    cs = ctx.reshape(n_batch * ctx_len, d)
    row2 = lambda a: a.reshape(1, -1)

    for l in range(depth):
        last = l == depth - 1
        mod_l = mod[l]
        g1, g2 = row2(norm1_gain[l]), row2(norm2_gain[l])
        zblk = jnp.zeros((LANES, LANES), F32)
        wpool = jnp.stack([jnp.block([[w_pool[l, 2 * p], zblk], [zblk, w_pool[l, 2 * p + 1]]])
                           for p in range(len(POOL_WINDOWS) // 2)]).astype(BF16)
        wsp = w_spatial[l].astype(BF16)
        pscale, sgg = row2(pool_scale[l]), row2(sg_v_gain[l])
        bsp = jnp.repeat(b_spatial[l].T, LANES, axis=1)
        sink_tab = jnp.repeat(attn_sink[l].reshape(2, GQA_GROUP), ATTN_BLOCK, axis=1)[:, None, :]

        common = (g1, wf, wpool, pscale, sgg, wsp, bsp)
        if last:
            kc, vc_t = _inproj(cs, mod_l, mod_row_c, *common, None, tm=tm_c, sub=tm_c, seq=ctx_len, dims=dims,
                               kv_only=True)
        else:
            ypc, qc_t, kc, vc_t, ysc = _inproj(cs, mod_l, mod_row_c, *common, None, tm=tm_c, sub=tm_c,
                                                 seq=ctx_len, dims=dims)
        ypx, qx_t, kx, vx_t, ysx = _inproj(xs, mod_l, mod_row_big, *common, rope_tabs, tm=big_x,
                                             sub=sub_x, seq=seq, dims=dims)
        merge_casts = [(w, l, all_cols(w)) for w in (w_br_pool, w_br_attn, w_br_sg, w_out)]
        ax, wbp, wba, wbs, wo = _window_attention(qx_t, kx, vx_t, kc, vc_t, sink_tab, merge_casts,
                                                  n_batch=n_batch, seq=seq, ctx_len=ctx_len)
        ffn_casts = [(w, l, all_cols(w)) for w in (w_ffn_in, w_ffn_out)]
        xs, wfi, wfo = _merge(xs, mod_l, mod_row_big, g1, ypx, ax, ysx, wg, wbp, wba, wbs, wo, ffn_casts,
                              tm=big_x, sub=sub_x)
        if last:
            xs, = _ffn(xs, mod_l, mod_row_big, g2, wfi, wfo, row2(final_gain), [], tm=big_x, sub=sub_x)
        else:
            ac = _context_attention(qc_t, kc, vc_t, sink_tab, n_batch=n_batch, ctx_len=ctx_len)
            cs, = _merge(cs, mod_l, mod_row_c, g1, ypc, ac, ysc, wg, wbp, wba, wbs, wo, [], tm=big_c, sub=sub_c)
            cs, = _ffn(cs, mod_l, mod_row_c, g2, wfi, wfo, None, [], tm=big_c, sub=sub_c)
            xs, wf, wg = _ffn(xs, mod_l, mod_row_big, g2, wfi, wfo, None, w_in_cast(l + 1), tm=big_x, sub=sub_x)
    return xs.reshape(n_batch, seq, d)
```
